```python
import math
import jax, jax.numpy as jnp
from jax import lax
import numpy as np

D_MODEL = 4096
BATCH = 4
SEQ = 2048
DEPTH = 4
DEC_BATCH = 8
DEC_SEQ = 4
PAST_LEN = 8192
PAGE_SIZE = 128

W_CONV = D_MODEL // 4
W_ATTN = D_MODEL // 4
W_POOL = D_MODEL // 4
W_SGU = D_MODEL - W_CONV - W_ATTN - W_POOL
MIX_WIDTH = W_CONV + W_ATTN + W_POOL + W_SGU
ATTN_HEAD_DIM = 128
N_ATTN_HEADS = W_ATTN // ATTN_HEAD_DIM
DK = ATTN_HEAD_DIM // 2
ROPE_DIM = DK // 4
ROPE_THETA = 500000.0
Q_BLOCK = 128
CONV_WIDTH = 31
POOL_WINDOWS = (2, 4, 8, 16)
N_POOL_GROUPS = len(POOL_WINDOWS)
POOL_GROUP = W_POOL // N_POOL_GROUPS
POOL_STATE = max(POOL_WINDOWS) - 1
CHUNK = 128
SGU_HEAD = 128
N_SGU_HEADS = W_SGU // SGU_HEAD
IN_SPLITS = (W_CONV, 2 * W_CONV, 2 * W_CONV + W_ATTN, 2 * W_CONV + 2 * W_ATTN,
             2 * W_CONV + 3 * W_ATTN, 2 * W_CONV + 3 * W_ATTN + W_POOL,
             2 * W_CONV + 3 * W_ATTN + W_POOL + W_SGU)
IN_COLS = IN_SPLITS[-1] + W_SGU
D_FF = 11008
N_EXPERTS = 8
TOP_K = 2
D_FF_EXPERT = 11008
N_DENSE = (DEPTH + 1) // 2
N_MOE = DEPTH // 2
DEEPNORM_ALPHA = (2 * DEPTH) ** 0.25
DEEPNORM_BETA = (8 * DEPTH) ** -0.25
LN_EPS = 1e-5
RMS_EPS = 1e-5

kernel_name = "hymba_style_diffattn_conformer_pool_sgu_deepnorm_step"


def layer_norm(x, g, b):
    xf = x.astype(jnp.float32)
    mu = jnp.mean(xf, axis=-1, keepdims=True)
    var = jnp.mean(jnp.square(xf - mu), axis=-1, keepdims=True)
    y = (xf - mu) * lax.rsqrt(var + LN_EPS) * g.astype(jnp.float32) + b.astype(jnp.float32)
    return y.astype(x.dtype)


def rms_norm(x, g):
    xf = x.astype(jnp.float32)
    y = xf * lax.rsqrt(jnp.mean(xf * xf, axis=-1, keepdims=True) + RMS_EPS) * g.astype(jnp.float32)
    return y.astype(x.dtype)


def rope(x, pos):
    half = ROPE_DIM // 2
    inv_freq = ROPE_THETA ** (-jnp.arange(half, dtype=jnp.float32) / half)
    ang = pos.astype(jnp.float32)[:, None] * inv_freq[None, :]
    cos = jnp.cos(ang)[None, :, None, None, :].astype(x.dtype)
    sin = jnp.sin(ang)[None, :, None, None, :].astype(x.dtype)
    x1, x2, rest = x[..., :half], x[..., half:ROPE_DIM], x[..., ROPE_DIM:]
    return jnp.concatenate([x1 * cos - x2 * sin, x2 * cos + x1 * sin, rest], axis=-1)


def diff_attend(q, k, v, q_pos, k_pos, lam):
    s = jnp.einsum('bqhcd,bkhcd->bhcqk', q, k, preferred_element_type=jnp.float32) * (1.0 / math.sqrt(DK))
    mask = k_pos[None, :] <= q_pos[:, None]
    s = jnp.where(mask, s, -jnp.inf)
    p = jax.nn.softmax(s, axis=-1)
    a = p[:, :, 0] - lam * p[:, :, 1]
    return jnp.einsum('bhqk,bkhd->bqhd', a.astype(v.dtype), v)


def diff_attention_prompt(q, k, v, pos, lam):
    B, T = q.shape[:2]
    nb = T // Q_BLOCK
    qb = q.reshape(B, nb, Q_BLOCK, N_ATTN_HEADS, 2, DK).swapaxes(0, 1)
    qpos = pos.reshape(nb, Q_BLOCK)

    def block(args):
        q_i, qpos_i = args
        return diff_attend(q_i, k, v, qpos_i, pos, lam)

    out = lax.map(block, (qb, qpos))
    return out.swapaxes(0, 1).reshape(B, T, N_ATTN_HEADS, ATTN_HEAD_DIM)


def token_mixers(x, pos, conv_prev, pool_prev, past, p, lam, lam_init):
    B, T, _ = x.shape
    h = x @ p['w_in']
    a_lin, a_gate, q, k, v, c_in, u, vs = jnp.split(h, IN_SPLITS, axis=-1)

    a = a_lin * jax.nn.sigmoid(a_gate)
    a_ext = jnp.concatenate([conv_prev.astype(a.dtype), a], axis=1)
    a_conv = lax.conv_general_dilated(
        a_ext, p['conv_w'][:, None, :].astype(a.dtype), window_strides=(1,), padding='VALID',
        dimension_numbers=('NWC', 'WIO', 'NWC'), feature_group_count=W_CONV) + p['conv_b']
    y_a = jax.nn.silu(layer_norm(a_conv, p['conv_ln_g'], p['conv_ln_b']))

    q = rope(q.reshape(B, T, N_ATTN_HEADS, 2, DK), pos)
    k = rope(k.reshape(B, T, N_ATTN_HEADS, 2, DK), pos)
    v = v.reshape(B, T, N_ATTN_HEADS, ATTN_HEAD_DIM)
    if past is None:
        o = diff_attention_prompt(q, k, v, pos, lam)
    else:
        cache_k_l, cache_v_l, page_table = past
        past_len = page_table.shape[1] * PAGE_SIZE
        k_past = cache_k_l[page_table].reshape(B, past_len, N_ATTN_HEADS, 2, DK)
        v_past = cache_v_l[page_table].reshape(B, past_len, N_ATTN_HEADS, ATTN_HEAD_DIM)
        k_all = jnp.concatenate([k_past.astype(k.dtype), k], axis=1)
        v_all = jnp.concatenate([v_past.astype(v.dtype), v], axis=1)
        k_pos = jnp.arange(past_len + T, dtype=jnp.int32)
        o = diff_attend(q, k_all, v_all, pos, k_pos, lam)
    o = rms_norm(o, p['subln_g']) * (1.0 - lam_init)
    y_b = o.reshape(B, T, W_ATTN)

    c_ext = jnp.concatenate([pool_prev.astype(c_in.dtype), c_in], axis=1)
    cs = jnp.cumsum(c_ext.astype(jnp.float32), axis=1)
    cs = jnp.concatenate([jnp.zeros_like(cs[:, :1]), cs], axis=1)
    s0 = POOL_STATE + 1
    pooled_groups = []
    for g, w in enumerate(POOL_WINDOWS):
        sl = slice(g * POOL_GROUP, (g + 1) * POOL_GROUP)
        win_sum = cs[:, s0:s0 + T, sl] - cs[:, s0 - w:s0 - w + T, sl]
        cnt = jnp.minimum(pos + 1, w).astype(jnp.float32)[None, :, None]
        pooled = (win_sum / cnt).astype(c_in.dtype) - c_in[..., sl]
        pooled_groups.append(pooled @ p['pool_w'][g])
    y_c = jnp.concatenate(pooled_groups, axis=-1) * p['pool_scale']

    vs = layer_norm(vs, p['sgu_ln_g'], p['sgu_ln_b'])
    L = min(T, CHUNK)
    nc = T // L
    vc = vs.reshape(B, nc, L, N_SGU_HEADS, SGU_HEAD)
    w_s = jnp.tril(p['sgu_w'][:, :L, :L])
    s = jnp.einsum('gij,bnjgc->bnigc', w_s, vc) + p['sgu_b'][:, :L].T[None, None, :, :, None]
    y_d = u * s.reshape(B, T, W_SGU)

    y = jnp.concatenate([y_a, y_b, y_c, y_d], axis=-1) @ p['w_out']
    new_state = (a_ext[:, -(CONV_WIDTH - 1):], c_ext[:, -POOL_STATE:],
                 k.reshape(B, T, N_ATTN_HEADS, 2 * DK), v, vs)
    return y, new_state


def swiglu(x, w1, w3, w2):
    return (jax.nn.silu(x @ w1) * (x @ w3)) @ w2


def moe_ffn(x, router_w, w1, w3, w2):
    xt = x.reshape(-1, x.shape[-1])
    logits = (xt @ router_w).astype(jnp.float32)
    top_v, top_i = lax.top_k(logits, TOP_K)
    gates = jax.nn.softmax(top_v, axis=-1)
    combine = jnp.sum(jax.nn.one_hot(top_i, N_EXPERTS, dtype=jnp.float32) * gates[..., None], axis=1)
    combine = combine.astype(x.dtype)
    y = jnp.zeros_like(xt)
    for e in range(N_EXPERTS):
        y = y + combine[:, e:e + 1] * swiglu(xt, w1[e], w3[e], w2[e])
    return y.reshape(x.shape)


def setup_inputs(seed: int = 0) -> dict:
    key = jax.random.key(seed)
    keys = jax.random.split(key, 40)
    f32 = jnp.float32

    def nrm(i, shape, scale=1.0):
        return jax.random.normal(keys[i], shape, f32) * scale

    n_pages = PAST_LEN // PAGE_SIZE
    n_used = DEC_BATCH * n_pages
    n_pool = n_used + (n_used + 3) // 4
    page_table = jax.random.permutation(keys[0], n_pool)[:n_used].reshape(DEC_BATCH, n_pages).astype(jnp.int32)
    L = DEPTH
    H = N_ATTN_HEADS
    return {
        'x_prompt': nrm(1, (BATCH, SEQ, D_MODEL)),
        'x_sample': nrm(2, (DEC_BATCH, DEC_SEQ, D_MODEL)),
        'cache_k': nrm(3, (L, n_pool, PAGE_SIZE, H, 2 * DK)),
        'cache_v': nrm(4, (L, n_pool, PAGE_SIZE, H, ATTN_HEAD_DIM)),
        'page_table': page_table,
        'state_conv': nrm(5, (L, DEC_BATCH, CONV_WIDTH - 1, W_CONV), 0.5),
        'state_pool': nrm(6, (L, DEC_BATCH, POOL_STATE, W_POOL)),
        'w_in': nrm(7, (L, D_MODEL, IN_COLS), D_MODEL ** -0.5),
        'w_out': nrm(8, (L, MIX_WIDTH, D_MODEL), DEEPNORM_BETA * MIX_WIDTH ** -0.5),
        'conv_w': nrm(9, (L, CONV_WIDTH, W_CONV), CONV_WIDTH ** -0.5),
        'conv_b': nrm(10, (L, W_CONV), 0.02),
        'conv_ln_g': 1.0 + nrm(11, (L, W_CONV), 0.02),
        'conv_ln_b': nrm(12, (L, W_CONV), 0.02),
        'lam_q1': nrm(13, (L, DK), 0.1),
        'lam_k1': nrm(14, (L, DK), 0.1),
        'lam_q2': nrm(15, (L, DK), 0.1),
        'lam_k2': nrm(16, (L, DK), 0.1),
        'subln_g': 1.0 + nrm(17, (L, ATTN_HEAD_DIM), 0.02),
        'pool_w': nrm(18, (L, N_POOL_GROUPS, POOL_GROUP, POOL_GROUP), POOL_GROUP ** -0.5),
        'pool_scale': 1.0 + nrm(19, (L, W_POOL), 0.02),
        'sgu_ln_g': 1.0 + nrm(20, (L, W_SGU), 0.02),
        'sgu_ln_b': nrm(21, (L, W_SGU), 0.02),
        'sgu_w': nrm(22, (L, N_SGU_HEADS, CHUNK, CHUNK), CHUNK ** -0.5),
        'sgu_b': 1.0 + nrm(23, (L, N_SGU_HEADS, CHUNK), 0.02),
        'ln1_g': 1.0 + nrm(24, (L, D_MODEL), 0.02),
        'ln1_b': nrm(25, (L, D_MODEL), 0.02),
        'ln2_g': 1.0 + nrm(26, (L, D_MODEL), 0.02),
        'ln2_b': nrm(27, (L, D_MODEL), 0.02),
        'ffn_w1': nrm(28, (N_DENSE, D_MODEL, D_FF), D_MODEL ** -0.5),
        'ffn_w3': nrm(29, (N_DENSE, D_MODEL, D_FF), D_MODEL ** -0.5),
        'ffn_w2': nrm(30, (N_DENSE, D_FF, D_MODEL), DEEPNORM_BETA * D_FF ** -0.5),
        'router_w': nrm(31, (N_MOE, D_MODEL, N_EXPERTS), D_MODEL ** -0.5),
        'moe_w1': nrm(32, (N_MOE, N_EXPERTS, D_MODEL, D_FF_EXPERT), D_MODEL ** -0.5),
        'moe_w3': nrm(33, (N_MOE, N_EXPERTS, D_MODEL, D_FF_EXPERT), D_MODEL ** -0.5),
        'moe_w2': nrm(34, (N_MOE, N_EXPERTS, D_FF_EXPERT, D_MODEL), DEEPNORM_BETA * D_FF_EXPERT ** -0.5),
    }


def reference(x_prompt, x_sample, cache_k, cache_v, page_table, state_conv, state_pool,
              w_in, w_out, conv_w, conv_b, conv_ln_g, conv_ln_b,
              lam_q1, lam_k1, lam_q2, lam_k2, subln_g, pool_w, pool_scale,
              sgu_ln_g, sgu_ln_b, sgu_w, sgu_b, ln1_g, ln1_b, ln2_g, ln2_b,
              ffn_w1, ffn_w3, ffn_w2, router_w, moe_w1, moe_w3, moe_w2):
    f32 = jnp.float32
    alpha = DEEPNORM_ALPHA
    B, T = x_prompt.shape[:2]
    Td = x_sample.shape[1]
    past_len = page_table.shape[1] * PAGE_SIZE
    pos_p = jnp.arange(T, dtype=jnp.int32)
    pos_s = past_len + jnp.arange(Td, dtype=jnp.int32)
    conv0 = jnp.zeros((B, CONV_WIDTH - 1, W_CONV), x_prompt.dtype)
    pool0 = jnp.zeros((B, POOL_STATE, W_POOL), x_prompt.dtype)

    xp, xs = x_prompt, x_sample
    kp_l, vp_l, ks_l, vs_l = [], [], [], []
    cp_l, cs_l, pp_l, ps_l, gs_l = [], [], [], [], []
    for l in range(DEPTH):
        lam_init = 0.8 - 0.6 * math.exp(-0.3 * l)
        lam = (jnp.exp(jnp.sum(lam_q1[l].astype(f32) * lam_k1[l].astype(f32)))
               - jnp.exp(jnp.sum(lam_q2[l].astype(f32) * lam_k2[l].astype(f32))) + lam_init)
        p = {'w_in': w_in[l], 'w_out': w_out[l], 'conv_w': conv_w[l], 'conv_b': conv_b[l],
             'conv_ln_g': conv_ln_g[l], 'conv_ln_b': conv_ln_b[l], 'subln_g': subln_g[l],
             'pool_w': pool_w[l], 'pool_scale': pool_scale[l], 'sgu_ln_g': sgu_ln_g[l],
             'sgu_ln_b': sgu_ln_b[l], 'sgu_w': sgu_w[l], 'sgu_b': sgu_b[l]}

        hp, (cp, pp, kp, vp, _) = token_mixers(xp, pos_p, conv0, pool0, None, p, lam, lam_init)
        hs, (cs, ps, ks, vs, gs) = token_mixers(xs, pos_s, state_conv[l], state_pool[l],
                                                (cache_k[l], cache_v[l], page_table), p, lam, lam_init)
        kp_l.append(kp); vp_l.append(vp); ks_l.append(ks); vs_l.append(vs)
        cp_l.append(cp); cs_l.append(cs); pp_l.append(pp); ps_l.append(ps); gs_l.append(gs)

        xp = layer_norm(alpha * xp + hp, ln1_g[l], ln1_b[l])
        xs = layer_norm(alpha * xs + hs, ln1_g[l], ln1_b[l])

        i = l // 2
        if l % 2 == 0:
            fp = swiglu(xp, ffn_w1[i], ffn_w3[i], ffn_w2[i])
            fs = swiglu(xs, ffn_w1[i], ffn_w3[i], ffn_w2[i])
        else:
            fp = moe_ffn(xp, router_w[i], moe_w1[i], moe_w3[i], moe_w2[i])
            fs = moe_ffn(xs, router_w[i], moe_w1[i], moe_w3[i], moe_w2[i])
        xp = layer_norm(alpha * xp + fp, ln2_g[l], ln2_b[l])
        xs = layer_norm(alpha * xs + fs, ln2_g[l], ln2_b[l])

    return (xp, xs,
            jnp.stack(kp_l), jnp.stack(vp_l), jnp.stack(ks_l), jnp.stack(vs_l),
            jnp.stack(cp_l), jnp.stack(cs_l), jnp.stack(pp_l), jnp.stack(ps_l),
            jnp.stack(gs_l))
```

```python
import functools
import math

import jax
import jax.numpy as jnp
from jax import lax
from jax.experimental import pallas as pl
from jax.experimental.pallas import tpu as pltpu

F32 = jnp.float32
BF16 = jnp.bfloat16

LN_EPS = 1e-5
RMS_EPS = 1e-5
ROPE_THETA = 500000.0
POOL_WINDOWS = (2, 4, 8, 16)
LANES = 128
SUBLANES = 8
BF16_SUBLANES = 16
ROW_PAD = 256
TIME_TILES = (256, 128, BF16_SUBLANES)
MIB = 1024 * 1024


def _round_up(x, m):
    return (x + m - 1) // m * m


def _params(semantics, vmem_mib):
    return pltpu.CompilerParams(dimension_semantics=semantics, vmem_limit_bytes=vmem_mib * MIB)


def _pick(n, candidates):
    for c in candidates:
        if n % c == 0:
            return c
    raise ValueError(f"no tile in {candidates} divides {n}")


def _mm_kernel(a_ref, w_ref, o_ref, wb_ref):
    @pl.when(pl.program_id(1) == 0)
    def _():
        wb_ref[...] = w_ref[...].astype(BF16)

    o_ref[...] = jnp.dot(a_ref[...], wb_ref[...], preferred_element_type=F32)


def _matmul(a, w, layer, *, bn=512, name):
    m, k = a.shape
    n = w.shape[-1]
    bm = _pick(m, (768, 512, 256))
    return pl.pallas_call(
        _mm_kernel,
        grid=(n // bn, m // bm),
        in_specs=[pl.BlockSpec((bm, k), lambda j, i: (i, 0)),
                  pl.BlockSpec((None, k, bn), lambda j, i: (layer, 0, j))],
        out_specs=pl.BlockSpec((bm, bn), lambda j, i: (i, j)),
        out_shape=jax.ShapeDtypeStruct((m, n), F32),
        scratch_shapes=[pltpu.VMEM((k, bn), BF16)],
        compiler_params=_params(("arbitrary", "arbitrary"), 52),
        name=name,
    )(a, w)


def _add_ln_kernel(x_ref, y_ref, g_ref, b_ref, of_ref, ob_ref, *, alpha):
    z = alpha * x_ref[...] + y_ref[...]
    mu = jnp.mean(z, axis=-1, keepdims=True)
    zc = z - mu
    var = jnp.mean(zc * zc, axis=-1, keepdims=True)
    o = zc * lax.rsqrt(var + LN_EPS) * g_ref[...] + b_ref[...]
    of_ref[...] = o
    ob_ref[...] = o.astype(BF16)


def _add_ln(x, y, g, b, alpha, *, name):
    m, d = x.shape
    tm = ROW_PAD
    row = pl.BlockSpec((tm, d), lambda i: (i, 0))
    vec = pl.BlockSpec((1, d), lambda i: (0, 0))
    return pl.pallas_call(
        functools.partial(_add_ln_kernel, alpha=alpha),
        grid=(m // tm,),
        in_specs=[row, row, vec, vec],
        out_specs=[row, row],
        out_shape=[jax.ShapeDtypeStruct((m, d), F32), jax.ShapeDtypeStruct((m, d), BF16)],
        compiler_params=_params(("arbitrary",), 48),
        name=name,
    )(x, y, g.reshape(1, d), b.reshape(1, d))


def _router_kernel(x_ref, w_ref, idx_ref, gate_ref, *, n_experts):
    logits = jnp.dot(x_ref[...], w_ref[...], preferred_element_type=F32,
                     precision=lax.Precision.HIGHEST)
    lane = lax.broadcasted_iota(jnp.int32, logits.shape, 1)
    lg = jnp.where(lane < n_experts, logits, -jnp.inf)
    m1 = jnp.max(lg, axis=-1, keepdims=True)
    i1 = jnp.min(jnp.where(lg == m1, lane, LANES), axis=-1, keepdims=True)
    lg2 = jnp.where(lane == i1, -jnp.inf, lg)
    m2 = jnp.max(lg2, axis=-1, keepdims=True)
    i2 = jnp.min(jnp.where(lg2 == m2, lane, LANES), axis=-1, keepdims=True)
    e2 = jnp.exp(m2 - m1)
    den = 1.0 + e2
    idx_ref[...] = jnp.where(lane == 0, i1, jnp.where(lane == 1, i2, 0))
    gate_ref[...] = jnp.where(lane == 0, 1.0 / den, jnp.where(lane == 1, e2 / den, 0.0))


def _router(x, router_w, *, name):
    m, d = x.shape
    e = router_w.shape[-1]
    w = jnp.zeros((d, LANES), F32).at[:, :e].set(router_w)
    tm = ROW_PAD
    return pl.pallas_call(
        functools.partial(_router_kernel, n_experts=e),
        grid=(m // tm,),
        in_specs=[pl.BlockSpec((tm, d), lambda i: (i, 0)), pl.BlockSpec((d, LANES), lambda i: (0, 0))],
        out_specs=[pl.BlockSpec((tm, LANES), lambda i: (i, 0))] * 2,
        out_shape=[jax.ShapeDtypeStruct((m, LANES), jnp.int32), jax.ShapeDtypeStruct((m, LANES), F32)],
        compiler_params=_params(("arbitrary",), 48),
        name=name,
    )(x, w)


FFN_SUB = 256
FFN_OUT_CHUNK = 512


def _ffn_kernel(te_ref, tr_ref, x_ref, w1_ref, w3_ref, w2_ref, g_ref, o_hbm,
                w1b, w3b, w2b, acc, sem, *, bm):
    i = pl.program_id(0)
    f = pl.program_id(1)
    rows = tr_ref[i]
    d = acc.shape[1]

    @pl.when(f == 0)
    def _():
        acc[...] = jnp.zeros_like(acc)

    @pl.when(rows > 0)
    def _():
        w1b[...] = w1_ref[...].astype(BF16)
        w3b[...] = w3_ref[...].astype(BF16)
        w2b[...] = w2_ref[...].astype(BF16)

    for r in range(bm // FFN_SUB):
        @pl.when(r * FFN_SUB < rows)
        def _():
            rs = pl.ds(r * FFN_SUB, FFN_SUB)
            xs = x_ref[rs, :]
            h1 = jnp.dot(xs, w1b[...], preferred_element_type=F32)
            h3 = jnp.dot(xs, w3b[...], preferred_element_type=F32)
            h = (h1 * jax.nn.sigmoid(h1) * h3).astype(BF16)
            for c in range(0, d, FFN_OUT_CHUNK):
                cs = pl.ds(c, FFN_OUT_CHUNK)
                acc[rs, cs] += jnp.dot(h, w2b[:, cs], preferred_element_type=F32)

    @pl.when(f == pl.num_programs(1) - 1)
    def _():
        acc[...] = acc[...] * g_ref[...]
        cp = pltpu.make_async_copy(acc, o_hbm.at[pl.ds(i * bm, bm), :], sem)
        cp.start()
        cp.wait()


def _ffn(x, gate, tile_expert, tile_rows, w1, w3, w2, *, bm, name):
    n_tiles = tile_expert.shape[0]
    d = x.shape[1]
    ff = w1.shape[-1]
    bf = 256
    nf = ff // bf
    assert ff % bf == 0 and gate.shape == (n_tiles * bm, 1)

    def w13_map(i, f, te, tr):
        return (te[i], 0, jnp.where(tr[i] > 0, f, nf - 1))

    def w2_map(i, f, te, tr):
        return (te[i], jnp.where(tr[i] > 0, f, nf - 1), 0)

    grid_spec = pltpu.PrefetchScalarGridSpec(
        num_scalar_prefetch=2,
        grid=(n_tiles, nf),
        in_specs=[
            pl.BlockSpec((bm, d), lambda i, f, te, tr: (i, 0), pipeline_mode=pl.Buffered(1)),
            pl.BlockSpec((None, d, bf), w13_map),
            pl.BlockSpec((None, d, bf), w13_map),
            pl.BlockSpec((None, bf, d), w2_map),
            pl.BlockSpec((bm, 1), lambda i, f, te, tr: (i, 0)),
        ],
        out_specs=pl.BlockSpec(memory_space=pl.ANY),
        scratch_shapes=[pltpu.VMEM((d, bf), BF16), pltpu.VMEM((d, bf), BF16), pltpu.VMEM((bf, d), BF16),
                        pltpu.VMEM((bm, d), F32), pltpu.SemaphoreType.DMA(())],
    )
    return pl.pallas_call(
        functools.partial(_ffn_kernel, bm=bm),
        grid_spec=grid_spec,
        out_shape=jax.ShapeDtypeStruct((n_tiles * bm, d), F32),
        compiler_params=_params(("arbitrary", "arbitrary"), 60),
        name=name,
    )(tile_expert, tile_rows, x, w1, w3, w2, gate)


def _rope_kernel(q_ref, k_ref, v_ref, cos_ref, sup_ref, sdn_ref, qz_ref, kb_ref, vb_ref, kr_ref,
                 *, dk, scale):
    def rot(x):
        return (x * cos_ref[...] + pltpu.roll(x, LANES - 8, 1) * sup_ref[...]
                + pltpu.roll(x, 8, 1) * sdn_ref[...])

    q = rot(q_ref[...]) * scale
    k = rot(k_ref[...])
    lane = lax.broadcasted_iota(jnp.int32, q.shape, 1)
    qz_ref[0] = jnp.where(lane < dk, q, 0.0).astype(BF16)
    qz_ref[1] = jnp.where(lane >= dk, q, 0.0).astype(BF16)
    kr_ref[...] = k
    kb_ref[...] = k.astype(BF16)
    vb_ref[...] = v_ref[...].astype(BF16)


def _rope_tables(pos, dk):
    rope_dim = dk // 4
    half = rope_dim // 2
    inv_freq = ROPE_THETA ** (-jnp.arange(half, dtype=F32) / half)
    ang = pos.astype(F32)[:, None] * inv_freq[None, :]
    cos, sin = jnp.cos(ang), jnp.sin(ang)
    t = pos.shape[0]
    ones = jnp.ones((t, dk - rope_dim), F32)
    zeros = jnp.zeros((t, dk - rope_dim), F32)
    zh = jnp.zeros((t, half), F32)
    c = jnp.concatenate([cos, cos, ones], axis=1)
    up = jnp.concatenate([-sin, zh, zeros], axis=1)
    dn = jnp.concatenate([zh, sin, zeros], axis=1)
    return tuple(jnp.concatenate([a, a], axis=1) for a in (c, up, dn))


def _rope(h, nb, t, w, pos, dk, *, name):
    heads = w // LANES
    tt = _pick(t, TIME_TILES)
    nt = t // tt
    cos, sup, sdn = _rope_tables(pos, dk)
    qcol, kcol, vcol = 2 * w // LANES, 3 * w // LANES, 4 * w // LANES
    tab = pl.BlockSpec((tt, LANES), lambda b, i, hd: (i, 0))
    return pl.pallas_call(
        functools.partial(_rope_kernel, dk=dk, scale=1.0 / math.sqrt(dk)),
        grid=(nb, nt, heads),
        in_specs=[pl.BlockSpec((tt, LANES), lambda b, i, hd: (b * nt + i, qcol + hd)),
                  pl.BlockSpec((tt, LANES), lambda b, i, hd: (b * nt + i, kcol + hd)),
                  pl.BlockSpec((tt, LANES), lambda b, i, hd: (b * nt + i, vcol + hd)),
                  tab, tab, tab],
        out_specs=[pl.BlockSpec((None, None, 2, tt, LANES), lambda b, i, hd: (b, hd, 0, i, 0)),
                   pl.BlockSpec((None, None, tt, LANES), lambda b, i, hd: (b, hd, i, 0)),
                   pl.BlockSpec((None, None, tt, LANES), lambda b, i, hd: (b, hd, i, 0)),
                   pl.BlockSpec((tt, LANES), lambda b, i, hd: (b * nt + i, hd))],
        out_shape=[jax.ShapeDtypeStruct((nb, heads, 2, t, LANES), BF16),
                   jax.ShapeDtypeStruct((nb, heads, t, LANES), BF16),
                   jax.ShapeDtypeStruct((nb, heads, t, LANES), BF16),
                   jax.ShapeDtypeStruct((nb * t, w), F32)],
        compiler_params=_params(("arbitrary", "arbitrary", "arbitrary"), 32),
        name=name,
    )(h, h, h, cos, sup, sdn)


def _lambda(lq1_ref, lk1_ref, lq2_ref, lk2_ref, lam_init):
    s1 = jnp.sum(lq1_ref[...] * lk1_ref[...], axis=-1, keepdims=True)
    s2 = jnp.sum(lq2_ref[...] * lk2_ref[...], axis=-1, keepdims=True)
    return jnp.exp(s1) - jnp.exp(s2) + lam_init


def _sub_ln(o, g, lam_init):
    return o * lax.rsqrt(jnp.mean(o * o, axis=-1, keepdims=True) + RMS_EPS) * g * (1.0 - lam_init)


def _flash_kernel(q_ref, k_ref, v_ref, lq1, lk1, lq2, lk2, g_ref, o_ref, m_sc, l_sc, acc_sc,
                  *, blk, lam_init):
    qi = pl.program_id(2)
    ki = pl.program_id(3)

    @pl.when(ki == 0)
    def _():
        m_sc[...] = jnp.full_like(m_sc, -jnp.inf)
        l_sc[...] = jnp.zeros_like(l_sc)
        acc_sc[...] = jnp.zeros_like(acc_sc)

    @pl.when(ki <= qi)
    def _():
        q = q_ref[...].reshape(2 * blk, LANES)
        s = lax.dot_general(q, k_ref[...], (((1,), (1,)), ((), ())), preferred_element_type=F32)
        row = lax.broadcasted_iota(jnp.int32, s.shape, 0)
        row = jnp.where(row >= blk, row - blk, row)
        col = lax.broadcasted_iota(jnp.int32, s.shape, 1)
        s = jnp.where((col <= row) | (ki < qi), s, -jnp.inf)
        m_prev = m_sc[...]
        m_new = jnp.maximum(m_prev, jnp.max(s, axis=-1, keepdims=True))
        alpha = jnp.exp(m_prev - m_new)
        p = jnp.exp(s - m_new)
        l_sc[...] = alpha * l_sc[...] + jnp.sum(p, axis=-1, keepdims=True)
        acc_sc[...] = alpha * acc_sc[...] + jnp.dot(p.astype(BF16), v_ref[...], preferred_element_type=F32)
        m_sc[...] = m_new

    @pl.when(ki == pl.num_programs(3) - 1)
    def _():
        lam = _lambda(lq1, lk1, lq2, lk2, lam_init)
        o = acc_sc[...] / l_sc[...]
        o = o[:blk] - lam * o[blk:]
        o_ref[...] = _sub_ln(o, g_ref[...], lam_init).astype(BF16)


def _flash(qz, kb, vb, lam_vecs, subln_g, lam_init, *, name):
    nb, heads, _, t, _ = qz.shape
    blk = _pick(t, (512, 256, 128))
    nq = t // blk
    vec = pl.BlockSpec((1, lam_vecs[0].shape[-1]), lambda b, hd, qi, ki: (0, 0))
    kv = pl.BlockSpec((None, None, blk, LANES), lambda b, hd, qi, ki: (b, hd, jnp.minimum(ki, qi), 0))
    return pl.pallas_call(
        functools.partial(_flash_kernel, blk=blk, lam_init=lam_init),
        grid=(nb, heads, nq, nq),
        in_specs=[pl.BlockSpec((None, None, 2, blk, LANES), lambda b, hd, qi, ki: (b, hd, 0, qi, 0)),
                  kv, kv, vec, vec, vec, vec,
                  pl.BlockSpec((1, LANES), lambda b, hd, qi, ki: (0, 0))],
        out_specs=pl.BlockSpec((blk, LANES), lambda b, hd, qi, ki: (b * nq + qi, hd)),
        out_shape=jax.ShapeDtypeStruct((nb * t, heads * LANES), BF16),
        scratch_shapes=[pltpu.VMEM((2 * blk, 1), F32), pltpu.VMEM((2 * blk, 1), F32),
                        pltpu.VMEM((2 * blk, LANES), F32)],
        compiler_params=_params(("arbitrary",) * 4, 48),
        name=name,
    )(qz, kb, vb, *lam_vecs, subln_g.reshape(1, LANES))


def _decode_kernel(pt_ref, q_ref, kc_ref, vc_ref, kn_ref, vn_ref, lq1, lk1, lq2, lk2, g_ref, o_ref,
                   m_sc, l_sc, acc_sc, *, heads, tq, lam_init):
    p = pl.program_id(1)
    rows_per_head = 2 * tq

    @pl.when(p == 0)
    def _():
        m_sc[...] = jnp.full_like(m_sc, -jnp.inf)
        l_sc[...] = jnp.zeros_like(l_sc)
        acc_sc[...] = jnp.zeros_like(acc_sc)

    q = q_ref[...]

    def update(k, v, causal):
        s = lax.dot_general(q, k.astype(BF16), (((1,), (1,)), ((), ())), preferred_element_type=F32)
        row = lax.broadcasted_iota(jnp.int32, s.shape, 0)
        col = lax.broadcasted_iota(jnp.int32, s.shape, 1)
        ok = (row // rows_per_head) == (col % heads)
        if causal:
            ok = ok & ((col // heads) <= (row % tq))
        s = jnp.where(ok, s, -jnp.inf)
        m_prev = m_sc[...]
        m_new = jnp.maximum(m_prev, jnp.max(s, axis=-1, keepdims=True))
        alpha = jnp.exp(m_prev - m_new)
        pr = jnp.exp(s - m_new)
        l_sc[...] = alpha * l_sc[...] + jnp.sum(pr, axis=-1, keepdims=True)
        acc_sc[...] = alpha * acc_sc[...] + jnp.dot(pr.astype(BF16), v.astype(BF16),
                                                    preferred_element_type=F32)
        m_sc[...] = m_new

    page_rows = kc_ref.shape[0] * kc_ref.shape[1]
    update(kc_ref[...].reshape(page_rows, LANES), vc_ref[...].reshape(page_rows, LANES), False)

    @pl.when(p == pl.num_programs(1) - 1)
    def _():
        update(kn_ref[...], vn_ref[...], True)
        lam = _lambda(lq1, lk1, lq2, lk2, lam_init)
        o = (acc_sc[...] / l_sc[...]).reshape(heads, 2, tq, LANES)
        o = o[:, 0] - lam * o[:, 1]
        o_ref[...] = _sub_ln(o, g_ref[...], lam_init).astype(BF16)


def _decode(page_table, qz, cache_k, cache_v, layer, k_new, v_new, lam_vecs, subln_g, lam_init, *, name):
    nb, n_pages = page_table.shape
    heads = cache_k.shape[3]
    page = cache_k.shape[2]
    tq = qz.shape[1] // (2 * heads)
    rows = qz.shape[1]
    vec = pl.BlockSpec((1, lam_vecs[0].shape[-1]), lambda b, p, pt: (0, 0))
    cache = pl.BlockSpec((None, None, page, heads, LANES),
                         lambda b, p, pt: (layer, pt[b * n_pages + p], 0, 0, 0))
    new = pl.BlockSpec((None, tq * heads, LANES), lambda b, p, pt: (b, 0, 0))
    grid_spec = pltpu.PrefetchScalarGridSpec(
        num_scalar_prefetch=1,
        grid=(nb, n_pages),
        in_specs=[pl.BlockSpec((None, rows, LANES), lambda b, p, pt: (b, 0, 0)),
                  cache, cache, new, new, vec, vec, vec, vec,
                  pl.BlockSpec((1, LANES), lambda b, p, pt: (0, 0))],
        out_specs=pl.BlockSpec((None, heads, tq, LANES), lambda b, p, pt: (b, 0, 0, 0)),
        scratch_shapes=[pltpu.VMEM((rows, 1), F32), pltpu.VMEM((rows, 1), F32),
                        pltpu.VMEM((rows, LANES), F32)],
    )
    return pl.pallas_call(
        functools.partial(_decode_kernel, heads=heads, tq=tq, lam_init=lam_init),
        grid_spec=grid_spec,
        out_shape=jax.ShapeDtypeStruct((nb, heads, tq, LANES), BF16),
        compiler_params=_params(("arbitrary", "arbitrary"), 32),
        name=name,
    )(page_table.reshape(-1), qz, cache_k, cache_v, k_new, v_new, *lam_vecs, subln_g.reshape(1, LANES))


CONV_HALO = 32


def _conv_kernel(al_ref, ag_ref, st_ref, w_ref, b_ref, g_ref, be_ref, y_ref, so_ref, ext,
                 *, tt, taps, t_valid):
    t = pl.program_id(1)
    nt = pl.num_programs(1)

    @pl.when(t == 0)
    def _():
        ext[0:CONV_HALO, :] = st_ref[...]

    ext[CONV_HALO:CONV_HALO + tt, :] = al_ref[...] * jax.nn.sigmoid(ag_ref[...])
    off = CONV_HALO - (taps - 1)
    acc = ext[pl.ds(off, tt), :] * w_ref[0:1, :]
    for j in range(1, taps):
        acc = acc + ext[pl.ds(off + j, tt), :] * w_ref[j:j + 1, :]
    acc = acc + b_ref[...]
    mu = jnp.mean(acc, axis=-1, keepdims=True)
    zc = acc - mu
    var = jnp.mean(zc * zc, axis=-1, keepdims=True)
    z = zc * lax.rsqrt(var + LN_EPS) * g_ref[...] + be_ref[...]
    y_ref[...] = (z * jax.nn.sigmoid(z)).astype(BF16)

    @pl.when(t == nt - 1)
    def _():
        so_ref[...] = ext[pl.ds(t_valid, CONV_HALO), :]

    if tt >= CONV_HALO:
        @pl.when(t < nt - 1)
        def _():
            ext[0:CONV_HALO, :] = ext[tt:tt + CONV_HALO, :]


def _conv(h, nb, t, t_valid, w, state, conv_w, conv_b, ln_g, ln_b, *, name):
    taps = conv_w.shape[0]
    tt = _pick(t, TIME_TILES)
    nt = t // tt
    assert nt == 1 or tt >= CONV_HALO
    hist = taps - 1
    st = jnp.pad(state, ((0, 0), (CONV_HALO - hist, 0), (0, 0)))
    vec = pl.BlockSpec((1, w), lambda b, i: (0, 0))
    y, so = pl.pallas_call(
        functools.partial(_conv_kernel, tt=tt, taps=taps, t_valid=t_valid - (nt - 1) * tt),
        grid=(nb, nt),
        in_specs=[pl.BlockSpec((tt, w), lambda b, i: (b * nt + i, 0)),
                  pl.BlockSpec((tt, w), lambda b, i: (b * nt + i, 1)),
                  pl.BlockSpec((None, CONV_HALO, w), lambda b, i: (b, 0, 0)),
                  pl.BlockSpec((taps, w), lambda b, i: (0, 0)), vec, vec, vec],
        out_specs=[pl.BlockSpec((tt, w), lambda b, i: (b * nt + i, 0)),
                   pl.BlockSpec((None, CONV_HALO, w), lambda b, i: (b, 0, 0))],
        out_shape=[jax.ShapeDtypeStruct((nb * t, w), BF16),
                   jax.ShapeDtypeStruct((nb, CONV_HALO, w), F32)],
        scratch_shapes=[pltpu.VMEM((CONV_HALO + tt, w), F32)],
        compiler_params=_params(("arbitrary", "arbitrary"), 32),
        name=name,
    )(h, h, st, conv_w, conv_b.reshape(1, w), ln_g.reshape(1, w), ln_b.reshape(1, w))
    return y, so[:, CONV_HALO - hist:]


POOL_HALO = 16


def _pool_kernel(c_ref, st_ref, pw_ref, sc_ref, y_ref, so_ref, ext, *, tt, t_valid, pos0, group):
    t = pl.program_id(1)
    nt = pl.num_programs(1)

    @pl.when(t == 0)
    def _():
        ext[0:POOL_HALO, :] = st_ref[...]

    ext[POOL_HALO:POOL_HALO + tt, :] = c_ref[...]
    pos = pos0 + t * tt + lax.broadcasted_iota(jnp.int32, (tt, 1), 0)
    for g, win in enumerate(POOL_WINDOWS):
        cols = pl.ds(g * group, group)
        tot = ext[pl.ds(POOL_HALO, tt), cols]
        for k in range(1, win):
            tot = tot + ext[pl.ds(POOL_HALO - k, tt), cols]
        cnt = jnp.minimum(pos + 1, win).astype(F32)
        pooled = tot / cnt - c_ref[:, cols]
        out = jnp.dot(pooled.astype(BF16), pw_ref[g].astype(BF16), preferred_element_type=F32)
        y_ref[:, cols] = (out * sc_ref[:, cols]).astype(BF16)

    @pl.when(t == nt - 1)
    def _():
        so_ref[...] = ext[pl.ds(t_valid, POOL_HALO), :]

    if tt >= POOL_HALO:
        @pl.when(t < nt - 1)
        def _():
            ext[0:POOL_HALO, :] = ext[tt:tt + POOL_HALO, :]


def _pool(h, nb, t, t_valid, w, pos0, state, pool_w, pool_scale, *, name):
    tt = _pick(t, TIME_TILES)
    nt = t // tt
    assert nt == 1 or tt >= POOL_HALO
    hist = max(POOL_WINDOWS) - 1
    groups, group = pool_w.shape[0], pool_w.shape[1]
    assert groups == len(POOL_WINDOWS) and groups * group == w
    st = jnp.pad(state, ((0, 0), (POOL_HALO - hist, 0), (0, 0)))
    ccol = 5
    y, so = pl.pallas_call(
        functools.partial(_pool_kernel, tt=tt, t_valid=t_valid - (nt - 1) * tt, pos0=pos0, group=group),
        grid=(nb, nt),
        in_specs=[pl.BlockSpec((tt, w), lambda b, i: (b * nt + i, ccol)),
                  pl.BlockSpec((None, POOL_HALO, w), lambda b, i: (b, 0, 0)),
                  pl.BlockSpec((groups, group, group), lambda b, i: (0, 0, 0)),
                  pl.BlockSpec((1, w), lambda b, i: (0, 0))],
        out_specs=[pl.BlockSpec((tt, w), lambda b, i: (b * nt + i, 0)),
                   pl.BlockSpec((None, POOL_HALO, w), lambda b, i: (b, 0, 0))],
        out_shape=[jax.ShapeDtypeStruct((nb * t, w), BF16),
                   jax.ShapeDtypeStruct((nb, POOL_HALO, w), F32)],
        scratch_shapes=[pltpu.VMEM((POOL_HALO + tt, w), F32)],
        compiler_params=_params(("arbitrary", "arbitrary"), 32),
        name=name,
    )(h, st, pool_w, pool_scale.reshape(1, w))
    return y, so[:, POOL_HALO - hist:]


def _sgu_kernel(u_ref, v_ref, g_ref, b_ref, ws_ref, bs_ref, y_ref, vn_ref, *, chunk, head):
    v = v_ref[...]
    mu = jnp.mean(v, axis=-1, keepdims=True)
    zc = v - mu
    var = jnp.mean(zc * zc, axis=-1, keepdims=True)
    vn = zc * lax.rsqrt(var + LN_EPS) * g_ref[...] + b_ref[...]
    vn_ref[...] = vn
    row = lax.broadcasted_iota(jnp.int32, (chunk, chunk), 0)
    col = lax.broadcasted_iota(jnp.int32, (chunk, chunk), 1)
    for g in range(ws_ref.shape[0]):
        cols = pl.ds(g * head, head)
        wg = jnp.where(col <= row, ws_ref[g], 0.0)
        vg = vn[:, g * head:(g + 1) * head]
        if chunk % LANES == 0:
            s = jnp.dot(wg.astype(BF16), vg.astype(BF16), preferred_element_type=F32)
        else:
            s = wg[:, 0:1] * vg[0:1, :]
            for j in range(1, chunk):
                s = s + wg[:, j:j + 1] * vg[j:j + 1, :]
        s = s + bs_ref[:, g:g + 1]
        y_ref[:, cols] = (u_ref[:, cols] * s).astype(BF16)


def _sgu(h, nb, t, w, ln_g, ln_b, sgu_w, sgu_b, *, name):
    chunk = min(t, sgu_w.shape[-1])
    nc = t // chunk
    n_heads = sgu_w.shape[0]
    head = w // n_heads
    ws = sgu_w[:, :chunk, :chunk]
    bs = sgu_b[:, :chunk].T
    vec = pl.BlockSpec((1, w), lambda b, i: (0, 0))
    return pl.pallas_call(
        functools.partial(_sgu_kernel, chunk=chunk, head=head),
        grid=(nb, nc),
        in_specs=[pl.BlockSpec((chunk, w), lambda b, i: (b * nc + i, 6)),
                  pl.BlockSpec((chunk, w), lambda b, i: (b * nc + i, 7)),
                  vec, vec,
                  pl.BlockSpec((n_heads, chunk, chunk), lambda b, i: (0, 0, 0)),
                  pl.BlockSpec((chunk, n_heads), lambda b, i: (0, 0))],
        out_specs=[pl.BlockSpec((chunk, w), lambda b, i: (b * nc + i, 0))] * 2,
        out_shape=[jax.ShapeDtypeStruct((nb * t, w), BF16), jax.ShapeDtypeStruct((nb * t, w), F32)],
        compiler_params=_params(("arbitrary", "arbitrary"), 32),
        name=name,
    )(h, h, ln_g.reshape(1, w), ln_b.reshape(1, w), ws, bs)


def _dispatch_plan(top_i, gates, n_valid, n_experts, bm, n_tiles):
    flat_e = top_i[:n_valid].reshape(-1)
    flat_g = gates[:n_valid].reshape(-1)
    onehot = (flat_e[:, None] == jnp.arange(n_experts, dtype=jnp.int32)[None, :]).astype(jnp.int32)
    counts = jnp.sum(onehot, axis=0)
    rank = jnp.take_along_axis(jnp.cumsum(onehot, axis=0), flat_e[:, None], axis=1)[:, 0] - 1
    tiles_e = (counts + bm - 1) // bm
    tile_end = jnp.cumsum(tiles_e)
    tile_start = tile_end - tiles_e
    pos = tile_start[flat_e] * bm + rank
    src = jnp.zeros((n_tiles * bm,), jnp.int32).at[pos].set(
        jnp.arange(flat_e.shape[0], dtype=jnp.int32) // top_i.shape[1])
    gate_sorted = jnp.zeros((n_tiles * bm,), F32).at[pos].set(flat_g)
    tile = jnp.arange(n_tiles, dtype=jnp.int32)
    used = tile < tile_end[-1]
    tile_e = jnp.minimum(jnp.searchsorted(tile_end, jnp.minimum(tile, tile_end[-1] - 1), side="right"),
                         n_experts - 1).astype(jnp.int32)
    rows = jnp.clip(counts[tile_e] - (tile - tile_start[tile_e]) * bm, 0, bm)
    rows = jnp.where(used, rows, 0).astype(jnp.int32)
    return pos.reshape(n_valid, -1), src, gate_sorted[:, None], tile_e, rows


def kernel(x_prompt, x_sample, cache_k, cache_v, page_table, state_conv, state_pool, w_in, w_out, conv_w, conv_b, conv_ln_g, conv_ln_b, lam_q1, lam_k1, lam_q2, lam_k2, subln_g, pool_w, pool_scale, sgu_ln_g, sgu_ln_b, sgu_w, sgu_b, ln1_g, ln1_b, ln2_g, ln2_b, ffn_w1, ffn_w3, ffn_w2, router_w, moe_w1, moe_w3, moe_w2):
    nb, t, d = x_prompt.shape
    nbs, ts, _ = x_sample.shape
    depth = w_in.shape[0]
    w = d // 4
    assert w_in.shape[-1] == 8 * w and w % LANES == 0
    heads = w // LANES
    dk = lam_q1.shape[-1]
    assert 2 * dk == LANES and subln_g.shape[-1] == LANES
    n_experts = router_w.shape[-1]
    top_k = 2
    past_len = page_table.shape[1] * cache_k.shape[2]
    alpha = (2 * depth) ** 0.25
    tsp = _round_up(ts, BF16_SUBLANES)

    mp, ms = nb * t, nbs * ts
    m_valid = mp + ms
    m = _round_up(m_valid, ROW_PAD)
    x = jnp.concatenate([x_prompt.reshape(mp, d), x_sample.reshape(ms, d),
                         jnp.zeros((m - m_valid, d), F32)], axis=0)
    xb = x.astype(BF16)

    ffn_bm = 1024
    dense_tiles = pl.cdiv(m, ffn_bm)
    dense_rows = jnp.clip(m - jnp.arange(dense_tiles, dtype=jnp.int32) * ffn_bm, 0, ffn_bm).astype(jnp.int32)
    dense_gate = jnp.ones((dense_tiles * ffn_bm, 1), F32)
    moe_tiles = (top_k * m_valid + n_experts * (ffn_bm - 1)) // ffn_bm

    pos_p = jnp.arange(t, dtype=jnp.int32)
    pos_s = past_len + jnp.arange(tsp, dtype=jnp.int32)
    conv0 = jnp.zeros((nb, conv_w.shape[1] - 1, w), F32)
    pool0 = jnp.zeros((nb, max(POOL_WINDOWS) - 1, w), F32)

    outs = [[] for _ in range(9)]
    for l in range(depth):
        lam_init = 0.8 - 0.6 * math.exp(-0.3 * l)
        lam_vecs = tuple(v[l].reshape(1, dk) for v in (lam_q1, lam_k1, lam_q2, lam_k2))
        h = _matmul(xb, w_in, l, name=f"in_proj_{l}")

        ya_p, conv_p = _conv(h, nb, t, t, w, conv0, conv_w[l], conv_b[l], conv_ln_g[l], conv_ln_b[l],
                             name=f"conv_p_{l}")
        qz_p, kb_p, vb_p, kr_p = _rope(h, nb, t, w, pos_p, dk, name=f"rope_p_{l}")
        yb_p = _flash(qz_p, kb_p, vb_p, lam_vecs, subln_g[l], lam_init, name=f"attn_p_{l}")
        yc_p, pool_p = _pool(h, nb, t, t, w, 0, pool0, pool_w[l], pool_scale[l], name=f"pool_p_{l}")
        yd_p, _ = _sgu(h, nb, t, w, sgu_ln_g[l], sgu_ln_b[l], sgu_w[l], sgu_b[l], name=f"sgu_p_{l}")

        hs = jnp.pad(h[mp:m_valid].reshape(nbs, ts, 8 * w), ((0, 0), (0, tsp - ts), (0, 0)))
        hs = hs.reshape(nbs * tsp, 8 * w)
        ya_s, conv_s = _conv(hs, nbs, tsp, ts, w, state_conv[l], conv_w[l], conv_b[l], conv_ln_g[l],
                             conv_ln_b[l], name=f"conv_s_{l}")
        qz_s, _, _, kr_s = _rope(hs, nbs, tsp, w, pos_s, dk, name=f"rope_s_{l}")
        k_new = kr_s.reshape(nbs, tsp * heads, LANES)
        v_new = hs[:, 4 * w:5 * w].reshape(nbs, tsp * heads, LANES)
        o_s = _decode(page_table, qz_s.reshape(nbs, heads * 2 * tsp, LANES), cache_k, cache_v, l,
                      k_new, v_new, lam_vecs, subln_g[l], lam_init, name=f"attn_s_{l}")
        yb_s = o_s.transpose(0, 2, 1, 3).reshape(nbs * tsp, w)
        yc_s, pool_s = _pool(hs, nbs, tsp, ts, w, past_len, state_pool[l], pool_w[l], pool_scale[l],
                             name=f"pool_s_{l}")
        yd_s, vn_s = _sgu(hs, nbs, tsp, w, sgu_ln_g[l], sgu_ln_b[l], sgu_w[l], sgu_b[l], name=f"sgu_s_{l}")

        y_p = jnp.concatenate([ya_p, yb_p, yc_p, yd_p], axis=1)
        y_s = jnp.concatenate([ya_s, yb_s, yc_s, yd_s], axis=1).reshape(nbs, tsp, d)[:, :ts].reshape(ms, d)
        ycat = jnp.concatenate([y_p, y_s, jnp.zeros((m - m_valid, d), BF16)], axis=0)
        mix = _matmul(ycat, w_out, l, name=f"out_proj_{l}")
        x, xb = _add_ln(x, mix, ln1_g[l], ln1_b[l], alpha, name=f"ln1_{l}")

        i = l // 2
        if l % 2 == 0:
            tile_e = jnp.full((dense_tiles,), i, jnp.int32)
            f = _ffn(xb, dense_gate, tile_e, dense_rows, ffn_w1, ffn_w3, ffn_w2, bm=ffn_bm,
                     name=f"ffn_{l}")
        else:
            idx, gate = _router(x, router_w[i], name=f"router_{l}")
            pos, src, gate_sorted, tile_e, rows = _dispatch_plan(
                idx[:, :top_k], gate[:, :top_k], m_valid, n_experts, ffn_bm, moe_tiles)
            routed = _ffn(xb[src], gate_sorted, tile_e + i * n_experts, rows,
                          *(wt.reshape((-1,) + wt.shape[2:]) for wt in (moe_w1, moe_w3, moe_w2)),
                          bm=ffn_bm, name=f"moe_{l}")
            f = routed[pos[:, 0]] + routed[pos[:, 1]]
            f = jnp.concatenate([f, jnp.zeros((m - m_valid, d), F32)], axis=0)
        x, xb = _add_ln(x, f, ln2_g[l], ln2_b[l], alpha, name=f"ln2_{l}")

        kp = kr_p.reshape(nb, t, heads, LANES)
        vp = h[:mp, 4 * w:5 * w].reshape(nb, t, heads, LANES)
        ks = kr_s.reshape(nbs, tsp, heads, LANES)[:, :ts]
        vs = hs[:, 4 * w:5 * w].reshape(nbs, tsp, heads, LANES)[:, :ts]
        gs = vn_s.reshape(nbs, tsp, w)[:, :ts]
        for acc, val in zip(outs, (kp, vp, ks, vs, conv_p, conv_s, pool_p, pool_s, gs)):
            acc.append(val)

    return (x[:mp].reshape(nb, t, d), x[mp:m_valid].reshape(nbs, ts, d)) + tuple(jnp.stack(o) for o in outs)
```

```python
import functools
import math

import jax
import jax.numpy as jnp
from jax import lax
from jax.experimental import pallas as pl
from jax.experimental.pallas import tpu as pltpu

F32 = jnp.float32
BF16 = jnp.bfloat16

LN_EPS = 1e-5
RMS_EPS = 1e-5
ROPE_THETA = 500000.0
POOL_WINDOWS = (2, 4, 8, 16)
LANES = 128
SUBLANES = 8
BF16_SUBLANES = 16
ROW_PAD = 256
TIME_TILES = (256, 128, BF16_SUBLANES)
MIB = 1024 * 1024


def _round_up(x, m):
    return (x + m - 1) // m * m


def _params(semantics, vmem_mib):
    return pltpu.CompilerParams(dimension_semantics=semantics, vmem_limit_bytes=vmem_mib * MIB)


def _pick(n, candidates):
    for c in candidates:
        if n % c == 0:
            return c
    raise ValueError(f"no tile in {candidates} divides {n}")


def _mm_kernel(a_ref, w_ref, o_ref, wb_ref):
    @pl.when(pl.program_id(1) == 0)
    def _():
        wb_ref[...] = w_ref[...].astype(BF16)

    o_ref[...] = jnp.dot(a_ref[...], wb_ref[...], preferred_element_type=F32)


def _matmul(a, w, layer, *, bn=512, name):
    m, k = a.shape
    n = w.shape[-1]
    bm = _pick(m, (768, 512, 256))
    return pl.pallas_call(
        _mm_kernel,
        grid=(n // bn, m // bm),
        in_specs=[pl.BlockSpec((bm, k), lambda j, i: (i, 0)),
                  pl.BlockSpec((None, k, bn), lambda j, i: (layer, 0, j))],
        out_specs=pl.BlockSpec((bm, bn), lambda j, i: (i, j)),
        out_shape=jax.ShapeDtypeStruct((m, n), F32),
        scratch_shapes=[pltpu.VMEM((k, bn), BF16)],
        compiler_params=_params(("arbitrary", "arbitrary"), 52),
        name=name,
    )(a, w)


def _add_ln_kernel(x_ref, y_ref, g_ref, b_ref, of_ref, ob_ref, *, alpha):
    z = alpha * x_ref[...] + y_ref[...]
    mu = jnp.mean(z, axis=-1, keepdims=True)
    zc = z - mu
    var = jnp.mean(zc * zc, axis=-1, keepdims=True)
    o = zc * lax.rsqrt(var + LN_EPS) * g_ref[...] + b_ref[...]
    of_ref[...] = o
    ob_ref[...] = o.astype(BF16)


def _add_ln(x, y, g, b, alpha, *, name):
    m, d = x.shape
    tm = ROW_PAD
    row = pl.BlockSpec((tm, d), lambda i: (i, 0))
    vec = pl.BlockSpec((1, d), lambda i: (0, 0))
    return pl.pallas_call(
        functools.partial(_add_ln_kernel, alpha=alpha),
        grid=(m // tm,),
        in_specs=[row, row, vec, vec],
        out_specs=[row, row],
        out_shape=[jax.ShapeDtypeStruct((m, d), F32), jax.ShapeDtypeStruct((m, d), BF16)],
        compiler_params=_params(("arbitrary",), 48),
        name=name,
    )(x, y, g.reshape(1, d), b.reshape(1, d))


def _router_kernel(x_ref, w_ref, idx_ref, gate_ref, *, n_experts):
    logits = jnp.dot(x_ref[...], w_ref[...], preferred_element_type=F32,
                     precision=lax.Precision.HIGHEST)
    lane = lax.broadcasted_iota(jnp.int32, logits.shape, 1)
    lg = jnp.where(lane < n_experts, logits, -jnp.inf)
    m1 = jnp.max(lg, axis=-1, keepdims=True)
    i1 = jnp.min(jnp.where(lg == m1, lane, LANES), axis=-1, keepdims=True)
    lg2 = jnp.where(lane == i1, -jnp.inf, lg)
    m2 = jnp.max(lg2, axis=-1, keepdims=True)
    i2 = jnp.min(jnp.where(lg2 == m2, lane, LANES), axis=-1, keepdims=True)
    e2 = jnp.exp(m2 - m1)
    den = 1.0 + e2
    idx_ref[...] = jnp.where(lane == 0, i1, jnp.where(lane == 1, i2, 0))
    gate_ref[...] = jnp.where(lane == 0, 1.0 / den, jnp.where(lane == 1, e2 / den, 0.0))


def _router(x, router_w, *, name):
    m, d = x.shape
    e = router_w.shape[-1]
    w = jnp.zeros((d, LANES), F32).at[:, :e].set(router_w)
    tm = ROW_PAD
    return pl.pallas_call(
        functools.partial(_router_kernel, n_experts=e),
        grid=(m // tm,),
        in_specs=[pl.BlockSpec((tm, d), lambda i: (i, 0)), pl.BlockSpec((d, LANES), lambda i: (0, 0))],
        out_specs=[pl.BlockSpec((tm, LANES), lambda i: (i, 0))] * 2,
        out_shape=[jax.ShapeDtypeStruct((m, LANES), jnp.int32), jax.ShapeDtypeStruct((m, LANES), F32)],
        compiler_params=_params(("arbitrary",), 48),
        name=name,
    )(x, w)


FFN_SUB = 256
FFN_OUT_CHUNK = 512


def _ffn_kernel(te_ref, tr_ref, x_ref, w1_ref, w3_ref, w2_ref, g_ref, o_hbm,
                w1b, w3b, w2b, acc, sem, *, bm):
    i = pl.program_id(0)
    f = pl.program_id(1)
    rows = tr_ref[i]
    d = acc.shape[1]

    @pl.when(f == 0)
    def _():
        acc[...] = jnp.zeros_like(acc)

    def cast_weights():
        w1b[...] = w1_ref[...].astype(BF16)
        w3b[...] = w3_ref[...].astype(BF16)
        w2b[...] = w2_ref[...].astype(BF16)

    def swiglu_rows(rs):
        xs = x_ref[rs, :]
        h1 = jnp.dot(xs, w1b[...], preferred_element_type=F32)
        h3 = jnp.dot(xs, w3b[...], preferred_element_type=F32)
        h = (h1 * jax.nn.sigmoid(h1) * h3).astype(BF16)
        for c in range(0, d, FFN_OUT_CHUNK):
            cs = pl.ds(c, FFN_OUT_CHUNK)
            acc[rs, cs] += jnp.dot(h, w2b[:, cs], preferred_element_type=F32)

    @pl.when(rows > bm - FFN_SUB)
    def _():
        cast_weights()
        swiglu_rows(pl.ds(0, bm))

    @pl.when((rows > 0) & (rows <= bm - FFN_SUB))
    def _():
        cast_weights()
        for r in range(bm // FFN_SUB - 1):
            @pl.when(r * FFN_SUB < rows)
            def _():
                swiglu_rows(pl.ds(r * FFN_SUB, FFN_SUB))

    @pl.when(f == pl.num_programs(1) - 1)
    def _():
        acc[...] = acc[...] * g_ref[...]
        cp = pltpu.make_async_copy(acc, o_hbm.at[pl.ds(i * bm, bm), :], sem)
        cp.start()
        cp.wait()


def _ffn(x, gate, tile_expert, tile_rows, w1, w3, w2, *, bm, name):
    n_tiles = tile_expert.shape[0]
    d = x.shape[1]
    ff = w1.shape[-1]
    bf = 256
    nf = ff // bf
    assert ff % bf == 0 and gate.shape == (n_tiles * bm, 1)

    def w13_map(i, f, te, tr):
        return (te[i], 0, jnp.where(tr[i] > 0, f, nf - 1))

    def w2_map(i, f, te, tr):
        return (te[i], jnp.where(tr[i] > 0, f, nf - 1), 0)

    grid_spec = pltpu.PrefetchScalarGridSpec(
        num_scalar_prefetch=2,
        grid=(n_tiles, nf),
        in_specs=[
            pl.BlockSpec((bm, d), lambda i, f, te, tr: (i, 0), pipeline_mode=pl.Buffered(1)),
            pl.BlockSpec((None, d, bf), w13_map),
            pl.BlockSpec((None, d, bf), w13_map),
            pl.BlockSpec((None, bf, d), w2_map),
            pl.BlockSpec((bm, 1), lambda i, f, te, tr: (i, 0)),
        ],
        out_specs=pl.BlockSpec(memory_space=pl.ANY),
        scratch_shapes=[pltpu.VMEM((d, bf), BF16), pltpu.VMEM((d, bf), BF16), pltpu.VMEM((bf, d), BF16),
                        pltpu.VMEM((bm, d), F32), pltpu.SemaphoreType.DMA(())],
    )
    return pl.pallas_call(
        functools.partial(_ffn_kernel, bm=bm),
        grid_spec=grid_spec,
        out_shape=jax.ShapeDtypeStruct((n_tiles * bm, d), F32),
        compiler_params=_params(("arbitrary", "arbitrary"), 56),
        name=name,
    )(tile_expert, tile_rows, x, w1, w3, w2, gate)


def _rope_kernel(q_ref, k_ref, v_ref, cos_ref, sup_ref, sdn_ref, qz_ref, kb_ref, vb_ref, kr_ref,
                 *, dk, scale):
    def rot(x):
        return (x * cos_ref[...] + pltpu.roll(x, LANES - 8, 1) * sup_ref[...]
                + pltpu.roll(x, 8, 1) * sdn_ref[...])

    q = rot(q_ref[...]) * scale
    k = rot(k_ref[...])
    lane = lax.broadcasted_iota(jnp.int32, q.shape, 1)
    qz_ref[0] = jnp.where(lane < dk, q, 0.0).astype(BF16)
    qz_ref[1] = jnp.where(lane >= dk, q, 0.0).astype(BF16)
    kr_ref[...] = k
    kb_ref[...] = k.astype(BF16)
    vb_ref[...] = v_ref[...].astype(BF16)


def _rope_tables(pos, dk):
    rope_dim = dk // 4
    half = rope_dim // 2
    inv_freq = ROPE_THETA ** (-jnp.arange(half, dtype=F32) / half)
    ang = pos.astype(F32)[:, None] * inv_freq[None, :]
    cos, sin = jnp.cos(ang), jnp.sin(ang)
    t = pos.shape[0]
    ones = jnp.ones((t, dk - rope_dim), F32)
    zeros = jnp.zeros((t, dk - rope_dim), F32)
    zh = jnp.zeros((t, half), F32)
    c = jnp.concatenate([cos, cos, ones], axis=1)
    up = jnp.concatenate([-sin, zh, zeros], axis=1)
    dn = jnp.concatenate([zh, sin, zeros], axis=1)
    return tuple(jnp.concatenate([a, a], axis=1) for a in (c, up, dn))


def _rope(h, nb, t, w, pos, dk, *, name):
    heads = w // LANES
    tt = _pick(t, TIME_TILES)
    nt = t // tt
    cos, sup, sdn = _rope_tables(pos, dk)
    qcol, kcol, vcol = 2 * w // LANES, 3 * w // LANES, 4 * w // LANES
    tab = pl.BlockSpec((tt, LANES), lambda b, i, hd: (i, 0))
    return pl.pallas_call(
        functools.partial(_rope_kernel, dk=dk, scale=math.log2(math.e) / math.sqrt(dk)),
        grid=(nb, nt, heads),
        in_specs=[pl.BlockSpec((tt, LANES), lambda b, i, hd: (b * nt + i, qcol + hd)),
                  pl.BlockSpec((tt, LANES), lambda b, i, hd: (b * nt + i, kcol + hd)),
                  pl.BlockSpec((tt, LANES), lambda b, i, hd: (b * nt + i, vcol + hd)),
                  tab, tab, tab],
        out_specs=[pl.BlockSpec((None, None, 2, tt, LANES), lambda b, i, hd: (b, hd, 0, i, 0)),
                   pl.BlockSpec((None, None, tt, LANES), lambda b, i, hd: (b, hd, i, 0)),
                   pl.BlockSpec((None, None, tt, LANES), lambda b, i, hd: (b, hd, i, 0)),
                   pl.BlockSpec((tt, LANES), lambda b, i, hd: (b * nt + i, hd))],
        out_shape=[jax.ShapeDtypeStruct((nb, heads, 2, t, LANES), BF16),
                   jax.ShapeDtypeStruct((nb, heads, t, LANES), BF16),
                   jax.ShapeDtypeStruct((nb, heads, t, LANES), BF16),
                   jax.ShapeDtypeStruct((nb * t, w), F32)],
        compiler_params=_params(("arbitrary", "arbitrary", "arbitrary"), 32),
        name=name,
    )(h, h, h, cos, sup, sdn)


def _lambda(lq1_ref, lk1_ref, lq2_ref, lk2_ref, lam_init):
    s1 = jnp.sum(lq1_ref[...] * lk1_ref[...], axis=-1, keepdims=True)
    s2 = jnp.sum(lq2_ref[...] * lk2_ref[...], axis=-1, keepdims=True)
    return jnp.exp(s1) - jnp.exp(s2) + lam_init


def _online_softmax_step(s, v, m_sc, l_sc, acc_sc):
    cols = [s[:, j:j + LANES] for j in range(0, s.shape[1], LANES)]
    m_prev = m_sc[...]
    m_loc = functools.reduce(jnp.maximum, cols)
    m_new = jnp.maximum(m_prev, jnp.broadcast_to(jnp.max(m_loc, axis=-1, keepdims=True), m_prev.shape))
    alpha = jnp.exp2(m_prev - m_new)
    p = [jnp.exp2(c - m_new) for c in cols]
    l_sc[...] = alpha * l_sc[...] + functools.reduce(jnp.add, p)
    pb = jnp.concatenate([c.astype(BF16) for c in p], axis=1) if len(p) > 1 else p[0].astype(BF16)
    acc_sc[...] = alpha * acc_sc[...] + jnp.dot(pb, v, preferred_element_type=F32)
    m_sc[...] = m_new


def _sub_ln(o, g, lam_init):
    return o * lax.rsqrt(jnp.mean(o * o, axis=-1, keepdims=True) + RMS_EPS) * g * (1.0 - lam_init)


def _flash_kernel(q_ref, k_ref, v_ref, lq1, lk1, lq2, lk2, g_ref, o_ref, m_sc, l_sc, acc_sc,
                  *, blk, lam_init):
    qi = pl.program_id(2)
    ki = pl.program_id(3)

    @pl.when(ki == 0)
    def _():
        m_sc[...] = jnp.full_like(m_sc, -jnp.inf)
        l_sc[...] = jnp.zeros_like(l_sc)
        acc_sc[...] = jnp.zeros_like(acc_sc)

    def step(diagonal):
        q = q_ref[...].reshape(2 * blk, LANES)
        s = lax.dot_general(q, k_ref[...], (((1,), (1,)), ((), ())), preferred_element_type=F32)
        if diagonal:
            row = lax.broadcasted_iota(jnp.int32, s.shape, 0)
            row = jnp.where(row >= blk, row - blk, row)
            col = lax.broadcasted_iota(jnp.int32, s.shape, 1)
            s = jnp.where(col <= row, s, -jnp.inf)
        _online_softmax_step(s, v_ref[...], m_sc, l_sc, acc_sc)

    @pl.when(ki < qi)
    def _():
        step(False)

    @pl.when(ki == qi)
    def _():
        step(True)

    @pl.when(ki == pl.num_programs(3) - 1)
    def _():
        lam = _lambda(lq1, lk1, lq2, lk2, lam_init)
        o = acc_sc[...] / jnp.sum(l_sc[...], axis=-1, keepdims=True)
        o = o[:blk] - lam * o[blk:]
        o_ref[...] = _sub_ln(o, g_ref[...], lam_init).astype(BF16)


def _flash(qz, kb, vb, lam_vecs, subln_g, lam_init, *, name):
    nb, heads, _, t, _ = qz.shape
    blk = _pick(t, (512, 256, 128))
    nq = t // blk
    vec = pl.BlockSpec((1, lam_vecs[0].shape[-1]), lambda b, hd, qi, ki: (0, 0))
    kv = pl.BlockSpec((None, None, blk, LANES), lambda b, hd, qi, ki: (b, hd, jnp.minimum(ki, qi), 0))
    return pl.pallas_call(
        functools.partial(_flash_kernel, blk=blk, lam_init=lam_init),
        grid=(nb, heads, nq, nq),
        in_specs=[pl.BlockSpec((None, None, 2, blk, LANES), lambda b, hd, qi, ki: (b, hd, 0, qi, 0)),
                  kv, kv, vec, vec, vec, vec,
                  pl.BlockSpec((1, LANES), lambda b, hd, qi, ki: (0, 0))],
        out_specs=pl.BlockSpec((blk, LANES), lambda b, hd, qi, ki: (b * nq + qi, hd)),
        out_shape=jax.ShapeDtypeStruct((nb * t, heads * LANES), BF16),
        scratch_shapes=[pltpu.VMEM((2 * blk, LANES), F32)] * 3,
        compiler_params=_params(("arbitrary",) * 4, 48),
        name=name,
    )(qz, kb, vb, *lam_vecs, subln_g.reshape(1, LANES))


def _decode_kernel(pt_ref, q_ref, *refs, heads, tq, lam_init, pages):
    kc_refs, vc_refs = refs[:pages], refs[pages:2 * pages]
    kn_ref, vn_ref, bp_ref, bn_ref, lq1, lk1, lq2, lk2, g_ref, o_ref, m_sc, l_sc, acc_sc = refs[2 * pages:]
    p = pl.program_id(1)

    @pl.when(p == 0)
    def _():
        m_sc[...] = jnp.full_like(m_sc, -jnp.inf)
        l_sc[...] = jnp.zeros_like(l_sc)
        acc_sc[...] = jnp.zeros_like(acc_sc)

    q = q_ref[...]

    def update(k, v, bias):
        s = lax.dot_general(q, k.astype(BF16), (((1,), (1,)), ((), ())), preferred_element_type=F32)
        _online_softmax_step(s + bias, v.astype(BF16), m_sc, l_sc, acc_sc)

    page_rows = kc_refs[0].shape[0] * kc_refs[0].shape[1]
    for kc_ref, vc_ref in zip(kc_refs, vc_refs):
        update(kc_ref[...].reshape(page_rows, LANES), vc_ref[...].reshape(page_rows, LANES), bp_ref[...])

    @pl.when(p == pl.num_programs(1) - 1)
    def _():
        update(kn_ref[...], vn_ref[...], bn_ref[...])
        lam = _lambda(lq1, lk1, lq2, lk2, lam_init)
        o = (acc_sc[...] / jnp.sum(l_sc[...], axis=-1, keepdims=True)).reshape(heads, 2, tq, LANES)
        o = o[:, 0] - lam * o[:, 1]
        o_ref[...] = _sub_ln(o, g_ref[...], lam_init)


DECODE_PAGES_PER_STEP = 4


def _decode_bias(heads, tq, n_pos, causal):
    row = jnp.arange(heads * 2 * tq, dtype=jnp.int32)[:, None]
    col = jnp.arange(n_pos * heads, dtype=jnp.int32)[None, :]
    ok = (row // (2 * tq)) == (col % heads)
    if causal:
        ok = ok & ((col // heads) <= (row % tq))
    return jnp.where(ok, 0.0, -jnp.inf).astype(F32)


def _decode(page_table, qz, cache_k, cache_v, layer, k_new, v_new, lam_vecs, subln_g, lam_init, *, name):
    nb, n_pages = page_table.shape
    heads = cache_k.shape[3]
    page = cache_k.shape[2]
    tq = qz.shape[1] // (2 * heads)
    rows = qz.shape[1]
    vec = pl.BlockSpec((1, lam_vecs[0].shape[-1]), lambda b, p, pt: (0, 0))
    pages = _pick(n_pages, (DECODE_PAGES_PER_STEP, 2, 1))
    cache = [pl.BlockSpec((None, None, page, heads, LANES),
                          lambda b, p, pt, j=j: (layer, pt[b * n_pages + p * pages + j], 0, 0, 0))
             for j in range(pages)]
    new_rows = k_new.shape[1]
    assert new_rows % LANES == 0 and new_rows // heads >= tq
    new = pl.BlockSpec((None, new_rows, LANES), lambda b, p, pt: (b, 0, 0))
    grid_spec = pltpu.PrefetchScalarGridSpec(
        num_scalar_prefetch=1,
        grid=(nb, n_pages // pages),
        in_specs=[pl.BlockSpec((None, rows, LANES), lambda b, p, pt: (b, 0, 0)),
                  *cache, *cache, new, new,
                  pl.BlockSpec((rows, page * heads), lambda b, p, pt: (0, 0)),
                  pl.BlockSpec((rows, new_rows), lambda b, p, pt: (0, 0)),
                  vec, vec, vec, vec,
                  pl.BlockSpec((1, LANES), lambda b, p, pt: (0, 0))],
        out_specs=pl.BlockSpec((None, heads, tq, LANES), lambda b, p, pt: (b, 0, 0, 0)),
        scratch_shapes=[pltpu.VMEM((rows, LANES), F32)] * 3,
    )
    return pl.pallas_call(
        functools.partial(_decode_kernel, heads=heads, tq=tq, lam_init=lam_init, pages=pages),
        grid_spec=grid_spec,
        out_shape=jax.ShapeDtypeStruct((nb, heads, tq, LANES), F32),
        compiler_params=_params(("arbitrary", "arbitrary"), 32),
        name=name,
    )(page_table.reshape(-1), qz, *([cache_k] * pages), *([cache_v] * pages), k_new, v_new,
      _decode_bias(heads, tq, page, False), _decode_bias(heads, tq, new_rows // heads, True),
      *lam_vecs, subln_g.reshape(1, LANES))


CONV_HALO = 32


def _conv_kernel(al_ref, ag_ref, st_ref, w_ref, b_ref, g_ref, be_ref, y_ref, so_ref, ext,
                 *, tt, taps, t_valid):
    t = pl.program_id(1)
    nt = pl.num_programs(1)

    @pl.when(t == 0)
    def _():
        ext[0:CONV_HALO, :] = st_ref[...]

    ext[CONV_HALO:CONV_HALO + tt, :] = al_ref[...] * jax.nn.sigmoid(ag_ref[...])
    off = CONV_HALO - (taps - 1)
    acc = ext[pl.ds(off, tt), :] * w_ref[0:1, :]
    for j in range(1, taps):
        acc = acc + ext[pl.ds(off + j, tt), :] * w_ref[j:j + 1, :]
    acc = acc + b_ref[...]
    mu = jnp.mean(acc, axis=-1, keepdims=True)
    zc = acc - mu
    var = jnp.mean(zc * zc, axis=-1, keepdims=True)
    z = zc * lax.rsqrt(var + LN_EPS) * g_ref[...] + be_ref[...]
    y_ref[...] = (z * jax.nn.sigmoid(z)).astype(BF16)

    @pl.when(t == nt - 1)
    def _():
        so_ref[...] = ext[pl.ds(t_valid, CONV_HALO), :]

    if tt >= CONV_HALO:
        @pl.when(t < nt - 1)
        def _():
            ext[0:CONV_HALO, :] = ext[tt:tt + CONV_HALO, :]


def _conv(h, nb, t, t_valid, w, state, conv_w, conv_b, ln_g, ln_b, *, name):
    taps = conv_w.shape[0]
    tt = _pick(t, TIME_TILES)
    nt = t // tt
    assert nt == 1 or tt >= CONV_HALO
    hist = taps - 1
    st = jnp.pad(state, ((0, 0), (CONV_HALO - hist, 0), (0, 0)))
    vec = pl.BlockSpec((1, w), lambda b, i: (0, 0))
    y, so = pl.pallas_call(
        functools.partial(_conv_kernel, tt=tt, taps=taps, t_valid=t_valid - (nt - 1) * tt),
        grid=(nb, nt),
        in_specs=[pl.BlockSpec((tt, w), lambda b, i: (b * nt + i, 0)),
                  pl.BlockSpec((tt, w), lambda b, i: (b * nt + i, 1)),
                  pl.BlockSpec((None, CONV_HALO, w), lambda b, i: (b, 0, 0)),
                  pl.BlockSpec((taps, w), lambda b, i: (0, 0)), vec, vec, vec],
        out_specs=[pl.BlockSpec((tt, w), lambda b, i: (b * nt + i, 0)),
                   pl.BlockSpec((None, CONV_HALO, w), lambda b, i: (b, 0, 0))],
        out_shape=[jax.ShapeDtypeStruct((nb * t, w), BF16),
                   jax.ShapeDtypeStruct((nb, CONV_HALO, w), F32)],
        scratch_shapes=[pltpu.VMEM((CONV_HALO + tt, w), F32)],
        compiler_params=_params(("arbitrary", "arbitrary"), 32),
        name=name,
    )(h, h, st, conv_w, conv_b.reshape(1, w), ln_g.reshape(1, w), ln_b.reshape(1, w))
    return y, so[:, CONV_HALO - hist:]


POOL_HALO = 16


def _pool_kernel(c_ref, st_ref, pw_ref, sc_ref, y_ref, so_ref, ext, *, tt, t_valid, pos0, group):
    t = pl.program_id(1)
    nt = pl.num_programs(1)

    @pl.when(t == 0)
    def _():
        ext[0:POOL_HALO, :] = st_ref[...]

    ext[POOL_HALO:POOL_HALO + tt, :] = c_ref[...]
    pos = pos0 + t * tt + lax.broadcasted_iota(jnp.int32, (tt, 1), 0)
    for g, win in enumerate(POOL_WINDOWS):
        cols = pl.ds(g * group, group)
        tot = ext[pl.ds(POOL_HALO, tt), cols]
        for k in range(1, win):
            tot = tot + ext[pl.ds(POOL_HALO - k, tt), cols]
        cnt = jnp.minimum(pos + 1, win).astype(F32)
        pooled = tot / cnt - c_ref[:, cols]
        out = jnp.dot(pooled.astype(BF16), pw_ref[g].astype(BF16), preferred_element_type=F32)
        y_ref[:, cols] = (out * sc_ref[:, cols]).astype(BF16)

    @pl.when(t == nt - 1)
    def _():
        so_ref[...] = ext[pl.ds(t_valid, POOL_HALO), :]

    if tt >= POOL_HALO:
        @pl.when(t < nt - 1)
        def _():
            ext[0:POOL_HALO, :] = ext[tt:tt + POOL_HALO, :]


def _pool(h, nb, t, t_valid, w, pos0, state, pool_w, pool_scale, *, name):
    tt = _pick(t, TIME_TILES)
    nt = t // tt
    assert nt == 1 or tt >= POOL_HALO
    hist = max(POOL_WINDOWS) - 1
    groups, group = pool_w.shape[0], pool_w.shape[1]
    assert groups == len(POOL_WINDOWS) and groups * group == w
    st = jnp.pad(state, ((0, 0), (POOL_HALO - hist, 0), (0, 0)))
    ccol = 5
    y, so = pl.pallas_call(
        functools.partial(_pool_kernel, tt=tt, t_valid=t_valid - (nt - 1) * tt, pos0=pos0, group=group),
        grid=(nb, nt),
        in_specs=[pl.BlockSpec((tt, w), lambda b, i: (b * nt + i, ccol)),
                  pl.BlockSpec((None, POOL_HALO, w), lambda b, i: (b, 0, 0)),
                  pl.BlockSpec((groups, group, group), lambda b, i: (0, 0, 0)),
                  pl.BlockSpec((1, w), lambda b, i: (0, 0))],
        out_specs=[pl.BlockSpec((tt, w), lambda b, i: (b * nt + i, 0)),
                   pl.BlockSpec((None, POOL_HALO, w), lambda b, i: (b, 0, 0))],
        out_shape=[jax.ShapeDtypeStruct((nb * t, w), BF16),
                   jax.ShapeDtypeStruct((nb, POOL_HALO, w), F32)],
        scratch_shapes=[pltpu.VMEM((POOL_HALO + tt, w), F32)],
        compiler_params=_params(("arbitrary", "arbitrary"), 32),
        name=name,
    )(h, st, pool_w, pool_scale.reshape(1, w))
    return y, so[:, POOL_HALO - hist:]


def _sgu_kernel(u_ref, v_ref, g_ref, b_ref, ws_ref, bs_ref, y_ref, vn_ref, *, chunk, head):
    v = v_ref[...]
    mu = jnp.mean(v, axis=-1, keepdims=True)
    zc = v - mu
    var = jnp.mean(zc * zc, axis=-1, keepdims=True)
    vn = zc * lax.rsqrt(var + LN_EPS) * g_ref[...] + b_ref[...]
    vn_ref[...] = vn
    row = lax.broadcasted_iota(jnp.int32, (chunk, chunk), 0)
    col = lax.broadcasted_iota(jnp.int32, (chunk, chunk), 1)
    for g in range(ws_ref.shape[0]):
        cols = pl.ds(g * head, head)
        wg = jnp.where(col <= row, ws_ref[g], 0.0)
        vg = vn[:, g * head:(g + 1) * head]
        if chunk % LANES == 0:
            s = jnp.dot(wg.astype(BF16), vg.astype(BF16), preferred_element_type=F32)
        else:
            s = wg[:, 0:1] * vg[0:1, :]
            for j in range(1, chunk):
                s = s + wg[:, j:j + 1] * vg[j:j + 1, :]
        s = s + bs_ref[:, g:g + 1]
        y_ref[:, cols] = (u_ref[:, cols] * s).astype(BF16)


def _sgu(h, nb, t, w, ln_g, ln_b, sgu_w, sgu_b, *, name):
    chunk = min(t, sgu_w.shape[-1])
    nc = t // chunk
    n_heads = sgu_w.shape[0]
    head = w // n_heads
    ws = sgu_w[:, :chunk, :chunk]
    bs = sgu_b[:, :chunk].T
    vec = pl.BlockSpec((1, w), lambda b, i: (0, 0))
    return pl.pallas_call(
        functools.partial(_sgu_kernel, chunk=chunk, head=head),
        grid=(nb, nc),
        in_specs=[pl.BlockSpec((chunk, w), lambda b, i: (b * nc + i, 6)),
                  pl.BlockSpec((chunk, w), lambda b, i: (b * nc + i, 7)),
                  vec, vec,
                  pl.BlockSpec((n_heads, chunk, chunk), lambda b, i: (0, 0, 0)),
                  pl.BlockSpec((chunk, n_heads), lambda b, i: (0, 0))],
        out_specs=[pl.BlockSpec((chunk, w), lambda b, i: (b * nc + i, 0))] * 2,
        out_shape=[jax.ShapeDtypeStruct((nb * t, w), BF16), jax.ShapeDtypeStruct((nb * t, w), F32)],
        compiler_params=_params(("arbitrary", "arbitrary"), 32),
        name=name,
    )(h, h, ln_g.reshape(1, w), ln_b.reshape(1, w), ws, bs)


def _dispatch_plan(top_i, gates, n_valid, n_experts, bm, n_tiles):
    flat_e = top_i[:n_valid].reshape(-1)
    flat_g = gates[:n_valid].reshape(-1)
    onehot = (flat_e[:, None] == jnp.arange(n_experts, dtype=jnp.int32)[None, :]).astype(jnp.int32)
    counts = jnp.sum(onehot, axis=0)
    rank = jnp.take_along_axis(jnp.cumsum(onehot, axis=0), flat_e[:, None], axis=1)[:, 0] - 1
    tiles_e = (counts + bm - 1) // bm
    tile_end = jnp.cumsum(tiles_e)
    tile_start = tile_end - tiles_e
    pos = tile_start[flat_e] * bm + rank
    src = jnp.zeros((n_tiles * bm,), jnp.int32).at[pos].set(
        jnp.arange(flat_e.shape[0], dtype=jnp.int32) // top_i.shape[1])
    gate_sorted = jnp.zeros((n_tiles * bm,), F32).at[pos].set(flat_g)
    tile = jnp.arange(n_tiles, dtype=jnp.int32)
    used = tile < tile_end[-1]
    tile_e = jnp.minimum(jnp.searchsorted(tile_end, jnp.minimum(tile, tile_end[-1] - 1), side="right"),
                         n_experts - 1).astype(jnp.int32)
    rows = jnp.clip(counts[tile_e] - (tile - tile_start[tile_e]) * bm, 0, bm)
    rows = jnp.where(used, rows, 0).astype(jnp.int32)
    return pos.reshape(n_valid, -1), src, gate_sorted[:, None], tile_e, rows


def kernel(x_prompt, x_sample, cache_k, cache_v, page_table, state_conv, state_pool, w_in, w_out, conv_w, conv_b, conv_ln_g, conv_ln_b, lam_q1, lam_k1, lam_q2, lam_k2, subln_g, pool_w, pool_scale, sgu_ln_g, sgu_ln_b, sgu_w, sgu_b, ln1_g, ln1_b, ln2_g, ln2_b, ffn_w1, ffn_w3, ffn_w2, router_w, moe_w1, moe_w3, moe_w2):
    nb, t, d = x_prompt.shape
    nbs, ts, _ = x_sample.shape
    depth = w_in.shape[0]
    w = d // 4
    assert w_in.shape[-1] == 8 * w and w % LANES == 0
    heads = w // LANES
    dk = lam_q1.shape[-1]
    assert 2 * dk == LANES and subln_g.shape[-1] == LANES
    n_experts = router_w.shape[-1]
    top_k = 2
    past_len = page_table.shape[1] * cache_k.shape[2]
    alpha = (2 * depth) ** 0.25
    tsp = _round_up(ts, BF16_SUBLANES)
    tqd = _round_up(ts, SUBLANES)

    mp, ms = nb * t, nbs * ts
    m_valid = mp + ms
    m = _round_up(m_valid, ROW_PAD)
    x = jnp.concatenate([x_prompt.reshape(mp, d), x_sample.reshape(ms, d),
                         jnp.zeros((m - m_valid, d), F32)], axis=0)
    xb = x.astype(BF16)

    ffn_bm = _pick(m, (768, 512, 256))
    dense_tiles = m // ffn_bm
    dense_rows = jnp.clip(m - jnp.arange(dense_tiles, dtype=jnp.int32) * ffn_bm, 0, ffn_bm).astype(jnp.int32)
    dense_gate = jnp.ones((dense_tiles * ffn_bm, 1), F32)
    moe_tiles = (top_k * m_valid + n_experts * (ffn_bm - 1)) // ffn_bm

    pos_p = jnp.arange(t, dtype=jnp.int32)
    pos_s = past_len + jnp.arange(tsp, dtype=jnp.int32)
    conv0 = jnp.zeros((nb, conv_w.shape[1] - 1, w), F32)
    pool0 = jnp.zeros((nb, max(POOL_WINDOWS) - 1, w), F32)

    outs = [[] for _ in range(9)]
    for l in range(depth):
        lam_init = 0.8 - 0.6 * math.exp(-0.3 * l)
        lam_vecs = tuple(v[l].reshape(1, dk) for v in (lam_q1, lam_k1, lam_q2, lam_k2))
        h = _matmul(xb, w_in, l, name=f"in_proj_{l}")

        ya_p, conv_p = _conv(h, nb, t, t, w, conv0, conv_w[l], conv_b[l], conv_ln_g[l], conv_ln_b[l],
                             name=f"conv_p_{l}")
        qz_p, kb_p, vb_p, kr_p = _rope(h, nb, t, w, pos_p, dk, name=f"rope_p_{l}")
        yb_p = _flash(qz_p, kb_p, vb_p, lam_vecs, subln_g[l], lam_init, name=f"attn_p_{l}")
        yc_p, pool_p = _pool(h, nb, t, t, w, 0, pool0, pool_w[l], pool_scale[l], name=f"pool_p_{l}")
        yd_p, _ = _sgu(h, nb, t, w, sgu_ln_g[l], sgu_ln_b[l], sgu_w[l], sgu_b[l], name=f"sgu_p_{l}")

        hs = jnp.pad(h[mp:m_valid].reshape(nbs, ts, 8 * w), ((0, 0), (0, tsp - ts), (0, 0)))
        hs = hs.reshape(nbs * tsp, 8 * w)
        ya_s, conv_s = _conv(hs, nbs, tsp, ts, w, state_conv[l], conv_w[l], conv_b[l], conv_ln_g[l],
                             conv_ln_b[l], name=f"conv_s_{l}")
        qz_s, _, _, kr_s = _rope(hs, nbs, tsp, w, pos_s, dk, name=f"rope_s_{l}")
        k_new = kr_s.reshape(nbs, tsp * heads, LANES)
        v_new = hs[:, 4 * w:5 * w].reshape(nbs, tsp * heads, LANES)
        o_s = _decode(page_table, qz_s[:, :, :, :tqd].reshape(nbs, heads * 2 * tqd, LANES), cache_k, cache_v,
                      l, k_new, v_new, lam_vecs, subln_g[l], lam_init, name=f"attn_s_{l}")
        yb_s = o_s.transpose(0, 2, 1, 3).reshape(nbs, tqd, w)[:, :ts].astype(BF16)
        yc_s, pool_s = _pool(hs, nbs, tsp, ts, w, past_len, state_pool[l], pool_w[l], pool_scale[l],
                             name=f"pool_s_{l}")
        yd_s, vn_s = _sgu(hs, nbs, tsp, w, sgu_ln_g[l], sgu_ln_b[l], sgu_w[l], sgu_b[l], name=f"sgu_s_{l}")

        y_p = jnp.concatenate([ya_p, yb_p, yc_p, yd_p], axis=1)
        y_s = jnp.concatenate([ya_s.reshape(nbs, tsp, w)[:, :ts], yb_s, yc_s.reshape(nbs, tsp, w)[:, :ts],
                               yd_s.reshape(nbs, tsp, w)[:, :ts]], axis=-1).reshape(ms, d)
        ycat = jnp.concatenate([y_p, y_s, jnp.zeros((m - m_valid, d), BF16)], axis=0)
        mix = _matmul(ycat, w_out, l, name=f"out_proj_{l}")
        x, xb = _add_ln(x, mix, ln1_g[l], ln1_b[l], alpha, name=f"ln1_{l}")

        i = l // 2
        if l % 2 == 0:
            tile_e = jnp.full((dense_tiles,), i, jnp.int32)
            f = _ffn(xb, dense_gate, tile_e, dense_rows, ffn_w1, ffn_w3, ffn_w2, bm=ffn_bm,
                     name=f"ffn_{l}")
        else:
            idx, gate = _router(x, router_w[i], name=f"router_{l}")
            pos, src, gate_sorted, tile_e, rows = _dispatch_plan(
                idx[:, :top_k], gate[:, :top_k], m_valid, n_experts, ffn_bm, moe_tiles)
            routed = _ffn(xb[src], gate_sorted, tile_e + i * n_experts, rows,
                          *(wt.reshape((-1,) + wt.shape[2:]) for wt in (moe_w1, moe_w3, moe_w2)),
                          bm=ffn_bm, name=f"moe_{l}")
            f = routed[pos[:, 0]] + routed[pos[:, 1]]
            f = jnp.concatenate([f, jnp.zeros((m - m_valid, d), F32)], axis=0)
        x, xb = _add_ln(x, f, ln2_g[l], ln2_b[l], alpha, name=f"ln2_{l}")

        kp = kr_p.reshape(nb, t, heads, LANES)
        vp = h[:mp, 4 * w:5 * w].reshape(nb, t, heads, LANES)
        ks = kr_s.reshape(nbs, tsp, heads, LANES)[:, :ts]
        vs = hs[:, 4 * w:5 * w].reshape(nbs, tsp, heads, LANES)[:, :ts]
        gs = vn_s.reshape(nbs, tsp, w)[:, :ts]
        for acc, val in zip(outs, (kp, vp, ks, vs, conv_p, conv_s, pool_p, pool_s, gs)):
            acc.append(val)

    return (x[:mp].reshape(nb, t, d), x[mp:m_valid].reshape(nbs, ts, d)) + tuple(jnp.stack(o) for o in outs)
```

```python
import functools
import math

import jax
import jax.numpy as jnp
from jax import lax
from jax.experimental import pallas as pl
from jax.experimental.pallas import tpu as pltpu

F32 = jnp.float32
BF16 = jnp.bfloat16

LN_EPS = 1e-5
RMS_EPS = 1e-5
ROPE_THETA = 500000.0
POOL_WINDOWS = (2, 4, 8, 16)
LANES = 128
SUBLANES = 8
BF16_SUBLANES = 16
ROW_PAD = 256
TIME_TILES = (256, 128, BF16_SUBLANES)
MIB = 1024 * 1024


def _round_up(x, m):
    return (x + m - 1) // m * m


def _params(semantics, vmem_mib):
    return pltpu.CompilerParams(dimension_semantics=semantics, vmem_limit_bytes=vmem_mib * MIB)


def _pick(n, candidates):
    for c in candidates:
        if n % c == 0:
            return c
    raise ValueError(f"no tile in {candidates} divides {n}")


def _mm_kernel(a_ref, w_ref, o_ref, wb_ref):
    @pl.when(pl.program_id(1) == 0)
    def _():
        wb_ref[...] = w_ref[...].astype(BF16)

    o_ref[...] = jnp.dot(a_ref[...], wb_ref[...], preferred_element_type=F32)


def _matmul(a, w, layer, *, bn=512, name):
    m, k = a.shape
    n = w.shape[-1]
    bm = _pick(m, (768, 512, 256))
    return pl.pallas_call(
        _mm_kernel,
        grid=(n // bn, m // bm),
        in_specs=[pl.BlockSpec((bm, k), lambda j, i: (i, 0)),
                  pl.BlockSpec((None, k, bn), lambda j, i: (layer, 0, j))],
        out_specs=pl.BlockSpec((bm, bn), lambda j, i: (i, j)),
        out_shape=jax.ShapeDtypeStruct((m, n), F32),
        scratch_shapes=[pltpu.VMEM((k, bn), BF16)],
        compiler_params=_params(("arbitrary", "arbitrary"), 52),
        name=name,
    )(a, w)


def _add_ln_kernel(x_ref, *refs, alpha, n_terms, gated):
    y_refs = refs[:n_terms]
    gate_ref = refs[n_terms] if gated else None
    g_ref, b_ref, of_ref, ob_ref = refs[n_terms + gated:]
    y = None
    for k, y_ref in enumerate(y_refs):
        term = gate_ref[:, k:k + 1] * y_ref[...] if gated else y_ref[...]
        y = term if y is None else y + term
    z = alpha * x_ref[...] + y
    mu = jnp.mean(z, axis=-1, keepdims=True)
    zc = z - mu
    var = jnp.mean(zc * zc, axis=-1, keepdims=True)
    o = zc * lax.rsqrt(var + LN_EPS) * g_ref[...] + b_ref[...]
    of_ref[...] = o
    ob_ref[...] = o.astype(BF16)


def _add_ln(x, ys, g, b, alpha, *, gate=None, name):
    m, d = x.shape
    tm = ROW_PAD // 2 if gate is not None else ROW_PAD
    row = pl.BlockSpec((tm, d), lambda i: (i, 0))
    vec = pl.BlockSpec((1, d), lambda i: (0, 0))
    gate_args = [] if gate is None else [gate]
    gate_specs = [] if gate is None else [pl.BlockSpec((tm, LANES), lambda i: (i, 0))]
    return pl.pallas_call(
        functools.partial(_add_ln_kernel, alpha=alpha, n_terms=len(ys), gated=gate is not None),
        grid=(m // tm,),
        in_specs=[row] * (1 + len(ys)) + gate_specs + [vec, vec],
        out_specs=[row, row],
        out_shape=[jax.ShapeDtypeStruct((m, d), F32), jax.ShapeDtypeStruct((m, d), BF16)],
        compiler_params=_params(("arbitrary",), 48),
        name=name,
    )(x, *ys, *gate_args, g.reshape(1, d), b.reshape(1, d))


def _router_kernel(x_ref, w_ref, idx_ref, gate_ref, rank_ref, cnt_ref, run_sc, *, n_experts, n_valid):
    i = pl.program_id(0)
    tm = x_ref.shape[0]

    @pl.when(i == 0)
    def _():
        run_sc[...] = jnp.zeros_like(run_sc)

    logits = jnp.dot(x_ref[...], w_ref[...], preferred_element_type=F32,
                     precision=lax.Precision.HIGHEST)
    lane = lax.broadcasted_iota(jnp.int32, logits.shape, 1)
    lg = jnp.where(lane < n_experts, logits, -jnp.inf)
    m1 = jnp.max(lg, axis=-1, keepdims=True)
    i1 = jnp.min(jnp.where(lg == m1, lane, LANES), axis=-1, keepdims=True)
    lg2 = jnp.where(lane == i1, -jnp.inf, lg)
    m2 = jnp.max(lg2, axis=-1, keepdims=True)
    i2 = jnp.min(jnp.where(lg2 == m2, lane, LANES), axis=-1, keepdims=True)
    e2 = jnp.exp(m2 - m1)
    den = 1.0 + e2
    idx_ref[...] = jnp.where(lane == 0, i1, jnp.where(lane == 1, i2, 0))
    gate_ref[...] = jnp.where(lane == 0, 1.0 / den, jnp.where(lane == 1, e2 / den, 0.0))

    row = i * tm + lax.broadcasted_iota(jnp.int32, (tm, 1), 0)
    onehot = jnp.where(((lane == i1) | (lane == i2)) & (row < n_valid), 1.0, 0.0)
    r_i = lax.broadcasted_iota(jnp.int32, (tm, tm), 0)
    c_i = lax.broadcasted_iota(jnp.int32, (tm, tm), 1)
    earlier = jnp.where(c_i < r_i, 1.0, 0.0).astype(BF16)
    before = jnp.dot(earlier, onehot.astype(BF16), preferred_element_type=F32) + run_sc[0:1, :]
    rank1 = jnp.sum(jnp.where(lane == i1, before, 0.0), axis=-1, keepdims=True)
    rank2 = jnp.sum(jnp.where(lane == i2, before, 0.0), axis=-1, keepdims=True)
    rank_ref[...] = jnp.where(lane == 0, rank1, jnp.where(lane == 1, rank2, 0.0)).astype(jnp.int32)
    run_sc[...] = run_sc[...] + jnp.sum(onehot, axis=0, keepdims=True)
    cnt_ref[...] = run_sc[...]


def _router(x, router_w, n_valid, *, name):
    m, d = x.shape
    e = router_w.shape[-1]
    w = jnp.zeros((d, LANES), F32).at[:, :e].set(router_w)
    tm = ROW_PAD
    tok = pl.BlockSpec((tm, LANES), lambda i: (i, 0))
    idx, gate, rank, cnt = pl.pallas_call(
        functools.partial(_router_kernel, n_experts=e, n_valid=n_valid),
        grid=(m // tm,),
        in_specs=[pl.BlockSpec((tm, d), lambda i: (i, 0)), pl.BlockSpec((d, LANES), lambda i: (0, 0))],
        out_specs=[tok, tok, tok, pl.BlockSpec((SUBLANES, LANES), lambda i: (0, 0))],
        out_shape=[jax.ShapeDtypeStruct((m, LANES), jnp.int32), jax.ShapeDtypeStruct((m, LANES), F32),
                   jax.ShapeDtypeStruct((m, LANES), jnp.int32), jax.ShapeDtypeStruct((SUBLANES, LANES), F32)],
        scratch_shapes=[pltpu.VMEM((SUBLANES, LANES), F32)],
        compiler_params=_params(("arbitrary",), 48),
        name=name,
    )(x, w)
    return idx, gate, rank, cnt[0, :e].astype(jnp.int32)


FFN_SUB = 256
FFN_OUT_CHUNK = 512
FFN_MAX_TILE = 1088


def _ffn_kernel(te_ref, tr_ref, x_ref, w1_ref, w3_ref, w2_ref, o_hbm,
                w1b, w3b, w2b, acc, sem, *, bm):
    i = pl.program_id(0)
    f = pl.program_id(1)
    rows = tr_ref[i]
    d = acc.shape[1]

    @pl.when(f == 0)
    def _():
        acc[...] = jnp.zeros_like(acc)

    def cast_weights():
        w1b[...] = w1_ref[...].astype(BF16)
        w3b[...] = w3_ref[...].astype(BF16)
        w2b[...] = w2_ref[...].astype(BF16)

    def swiglu_rows(rs):
        xs = x_ref[rs, :]
        h1 = jnp.dot(xs, w1b[...], preferred_element_type=F32)
        h3 = jnp.dot(xs, w3b[...], preferred_element_type=F32)
        h = (h1 * jax.nn.sigmoid(h1) * h3).astype(BF16)
        for c in range(0, d, FFN_OUT_CHUNK):
            cs = pl.ds(c, FFN_OUT_CHUNK)
            acc[rs, cs] += jnp.dot(h, w2b[:, cs], preferred_element_type=F32)

    @pl.when(rows > bm - FFN_SUB)
    def _():
        cast_weights()
        swiglu_rows(pl.ds(0, bm))

    @pl.when((rows > 0) & (rows <= bm - FFN_SUB))
    def _():
        cast_weights()
        for r in range(pl.cdiv(bm - FFN_SUB, FFN_SUB)):
            @pl.when(r * FFN_SUB < rows)
            def _():
                swiglu_rows(pl.ds(r * FFN_SUB, FFN_SUB))

    @pl.when(f == pl.num_programs(1) - 1)
    def _():
        cp = pltpu.make_async_copy(acc, o_hbm.at[pl.ds(i * bm, bm), :], sem)
        cp.start()
        cp.wait()


def _ffn(x, tile_expert, tile_rows, w1, w3, w2, *, bm, name):
    n_tiles = tile_expert.shape[0]
    d = x.shape[1]
    ff = w1.shape[-1]
    bf = 256
    nf = ff // bf
    assert ff % bf == 0 and bm % BF16_SUBLANES == 0 and x.shape[0] >= n_tiles * bm

    def w13_map(i, f, te, tr):
        return (te[i], 0, jnp.where(tr[i] > 0, f, nf - 1))

    def w2_map(i, f, te, tr):
        return (te[i], jnp.where(tr[i] > 0, f, nf - 1), 0)

    grid_spec = pltpu.PrefetchScalarGridSpec(
        num_scalar_prefetch=2,
        grid=(n_tiles, nf),
        in_specs=[
            pl.BlockSpec((bm, d), lambda i, f, te, tr: (i, 0), pipeline_mode=pl.Buffered(1)),
            pl.BlockSpec((None, d, bf), w13_map),
            pl.BlockSpec((None, d, bf), w13_map),
            pl.BlockSpec((None, bf, d), w2_map),
        ],
        out_specs=pl.BlockSpec(memory_space=pl.ANY),
        scratch_shapes=[pltpu.VMEM((d, bf), BF16), pltpu.VMEM((d, bf), BF16), pltpu.VMEM((bf, d), BF16),
                        pltpu.VMEM((bm, d), F32), pltpu.SemaphoreType.DMA(())],
    )
    return pl.pallas_call(
        functools.partial(_ffn_kernel, bm=bm),
        grid_spec=grid_spec,
        out_shape=jax.ShapeDtypeStruct((n_tiles * bm, d), F32),
        compiler_params=_params(("arbitrary", "arbitrary"), 60),
        name=name,
    )(tile_expert, tile_rows, x, w1, w3, w2)


def _rope_kernel(q_ref, k_ref, v_ref, cos_ref, sup_ref, sdn_ref, qz_ref, kb_ref, vb_ref, kr_ref,
                 *, dk, scale):
    def rot(x):
        return (x * cos_ref[...] + pltpu.roll(x, LANES - 8, 1) * sup_ref[...]
                + pltpu.roll(x, 8, 1) * sdn_ref[...])

    q = rot(q_ref[...]) * scale
    k = rot(k_ref[...])
    lane = lax.broadcasted_iota(jnp.int32, q.shape, 1)
    qz_ref[0] = jnp.where(lane < dk, q, 0.0).astype(BF16)
    qz_ref[1] = jnp.where(lane >= dk, q, 0.0).astype(BF16)
    kr_ref[...] = k
    kb_ref[...] = k.astype(BF16)
    vb_ref[...] = v_ref[...].astype(BF16)


def _rope_tables(pos, dk):
    rope_dim = dk // 4
    half = rope_dim // 2
    inv_freq = ROPE_THETA ** (-jnp.arange(half, dtype=F32) / half)
    ang = pos.astype(F32)[:, None] * inv_freq[None, :]
    cos, sin = jnp.cos(ang), jnp.sin(ang)
    t = pos.shape[0]
    ones = jnp.ones((t, dk - rope_dim), F32)
    zeros = jnp.zeros((t, dk - rope_dim), F32)
    zh = jnp.zeros((t, half), F32)
    c = jnp.concatenate([cos, cos, ones], axis=1)
    up = jnp.concatenate([-sin, zh, zeros], axis=1)
    dn = jnp.concatenate([zh, sin, zeros], axis=1)
    return tuple(jnp.concatenate([a, a], axis=1) for a in (c, up, dn))


def _rope(h, nb, t, w, pos, dk, *, name):
    heads = w // LANES
    tt = _pick(t, TIME_TILES)
    nt = t // tt
    cos, sup, sdn = _rope_tables(pos, dk)
    qcol, kcol, vcol = 2 * w // LANES, 3 * w // LANES, 4 * w // LANES
    tab = pl.BlockSpec((tt, LANES), lambda b, i, hd: (i, 0))
    return pl.pallas_call(
        functools.partial(_rope_kernel, dk=dk, scale=math.log2(math.e) / math.sqrt(dk)),
        grid=(nb, nt, heads),
        in_specs=[pl.BlockSpec((tt, LANES), lambda b, i, hd: (b * nt + i, qcol + hd)),
                  pl.BlockSpec((tt, LANES), lambda b, i, hd: (b * nt + i, kcol + hd)),
                  pl.BlockSpec((tt, LANES), lambda b, i, hd: (b * nt + i, vcol + hd)),
                  tab, tab, tab],
        out_specs=[pl.BlockSpec((None, None, 2, tt, LANES), lambda b, i, hd: (b, hd, 0, i, 0)),
                   pl.BlockSpec((None, None, tt, LANES), lambda b, i, hd: (b, hd, i, 0)),
                   pl.BlockSpec((None, None, tt, LANES), lambda b, i, hd: (b, hd, i, 0)),
                   pl.BlockSpec((tt, LANES), lambda b, i, hd: (b * nt + i, hd))],
        out_shape=[jax.ShapeDtypeStruct((nb, heads, 2, t, LANES), BF16),
                   jax.ShapeDtypeStruct((nb, heads, t, LANES), BF16),
                   jax.ShapeDtypeStruct((nb, heads, t, LANES), BF16),
                   jax.ShapeDtypeStruct((nb * t, w), F32)],
        compiler_params=_params(("arbitrary", "arbitrary", "arbitrary"), 32),
        name=name,
    )(h, h, h, cos, sup, sdn)


def _lambda(lq1_ref, lk1_ref, lq2_ref, lk2_ref, lam_init):
    s1 = jnp.sum(lq1_ref[...] * lk1_ref[...], axis=-1, keepdims=True)
    s2 = jnp.sum(lq2_ref[...] * lk2_ref[...], axis=-1, keepdims=True)
    return jnp.exp(s1) - jnp.exp(s2) + lam_init


def _online_softmax_step(s, v, m_sc, l_sc, acc_sc):
    cols = [s[:, j:j + LANES] for j in range(0, s.shape[1], LANES)]
    m_prev = m_sc[...]
    m_loc = functools.reduce(jnp.maximum, cols)
    m_new = jnp.maximum(m_prev, jnp.broadcast_to(jnp.max(m_loc, axis=-1, keepdims=True), m_prev.shape))
    alpha = jnp.exp2(m_prev - m_new)
    p = [jnp.exp2(c - m_new) for c in cols]
    l_sc[...] = alpha * l_sc[...] + functools.reduce(jnp.add, p)
    pb = jnp.concatenate([c.astype(BF16) for c in p], axis=1) if len(p) > 1 else p[0].astype(BF16)
    acc_sc[...] = alpha * acc_sc[...] + jnp.dot(pb, v, preferred_element_type=F32)
    m_sc[...] = m_new


def _sub_ln(o, g, lam_init):
    return o * lax.rsqrt(jnp.mean(o * o, axis=-1, keepdims=True) + RMS_EPS) * g * (1.0 - lam_init)


def _flash_kernel(q_ref, k_ref, v_ref, lq1, lk1, lq2, lk2, g_ref, o_ref, m_sc, l_sc, acc_sc,
                  *, blk, lam_init):
    qi = pl.program_id(2)
    ki = pl.program_id(3)

    @pl.when(ki == 0)
    def _():
        m_sc[...] = jnp.full_like(m_sc, -jnp.inf)
        l_sc[...] = jnp.zeros_like(l_sc)
        acc_sc[...] = jnp.zeros_like(acc_sc)

    def step(diagonal):
        q = q_ref[...].reshape(2 * blk, LANES)
        s = lax.dot_general(q, k_ref[...], (((1,), (1,)), ((), ())), preferred_element_type=F32)
        if diagonal:
            row = lax.broadcasted_iota(jnp.int32, s.shape, 0)
            row = jnp.where(row >= blk, row - blk, row)
            col = lax.broadcasted_iota(jnp.int32, s.shape, 1)
            s = jnp.where(col <= row, s, -jnp.inf)
        _online_softmax_step(s, v_ref[...], m_sc, l_sc, acc_sc)

    @pl.when(ki < qi)
    def _():
        step(False)

    @pl.when(ki == qi)
    def _():
        step(True)

    @pl.when(ki == pl.num_programs(3) - 1)
    def _():
        lam = _lambda(lq1, lk1, lq2, lk2, lam_init)
        o = acc_sc[...] / jnp.sum(l_sc[...], axis=-1, keepdims=True)
        o = o[:blk] - lam * o[blk:]
        o_ref[...] = _sub_ln(o, g_ref[...], lam_init).astype(BF16)


def _flash(qz, kb, vb, lam_vecs, subln_g, lam_init, *, name):
    nb, heads, _, t, _ = qz.shape
    blk = _pick(t, (512, 256, 128))
    nq = t // blk
    vec = pl.BlockSpec((1, lam_vecs[0].shape[-1]), lambda b, hd, qi, ki: (0, 0))
    kv = pl.BlockSpec((None, None, blk, LANES), lambda b, hd, qi, ki: (b, hd, jnp.minimum(ki, qi), 0))
    return pl.pallas_call(
        functools.partial(_flash_kernel, blk=blk, lam_init=lam_init),
        grid=(nb, heads, nq, nq),
        in_specs=[pl.BlockSpec((None, None, 2, blk, LANES), lambda b, hd, qi, ki: (b, hd, 0, qi, 0)),
                  kv, kv, vec, vec, vec, vec,
                  pl.BlockSpec((1, LANES), lambda b, hd, qi, ki: (0, 0))],
        out_specs=pl.BlockSpec((blk, LANES), lambda b, hd, qi, ki: (b * nq + qi, hd)),
        out_shape=jax.ShapeDtypeStruct((nb * t, heads * LANES), BF16),
        scratch_shapes=[pltpu.VMEM((2 * blk, LANES), F32)] * 3,
        compiler_params=_params(("arbitrary",) * 4, 48),
        name=name,
    )(qz, kb, vb, *lam_vecs, subln_g.reshape(1, LANES))


def _decode_kernel(pt_ref, q_ref, *refs, heads, tq, lam_init, pages):
    kc_refs, vc_refs = refs[:pages], refs[pages:2 * pages]
    kn_ref, vn_ref, bp_ref, bn_ref, lq1, lk1, lq2, lk2, g_ref, o_ref, m_sc, l_sc, acc_sc = refs[2 * pages:]
    p = pl.program_id(1)

    @pl.when(p == 0)
    def _():
        m_sc[...] = jnp.full_like(m_sc, -jnp.inf)
        l_sc[...] = jnp.zeros_like(l_sc)
        acc_sc[...] = jnp.zeros_like(acc_sc)

    q = q_ref[...]

    def update(k, v, bias):
        s = lax.dot_general(q, k.astype(BF16), (((1,), (1,)), ((), ())), preferred_element_type=F32)
        _online_softmax_step(s + bias, v.astype(BF16), m_sc, l_sc, acc_sc)

    page_rows = kc_refs[0].shape[0] * kc_refs[0].shape[1]
    for kc_ref, vc_ref in zip(kc_refs, vc_refs):
        update(kc_ref[...].reshape(page_rows, LANES), vc_ref[...].reshape(page_rows, LANES), bp_ref[...])

    @pl.when(p == pl.num_programs(1) - 1)
    def _():
        update(kn_ref[...], vn_ref[...], bn_ref[...])
        lam = _lambda(lq1, lk1, lq2, lk2, lam_init)
        o = (acc_sc[...] / jnp.sum(l_sc[...], axis=-1, keepdims=True)).reshape(heads, 2, tq, LANES)
        o = o[:, 0] - lam * o[:, 1]
        o_ref[...] = _sub_ln(o, g_ref[...], lam_init)


DECODE_PAGES_PER_STEP = 4


def _decode_bias(heads, tq, n_pos, causal):
    row = jnp.arange(heads * 2 * tq, dtype=jnp.int32)[:, None]
    col = jnp.arange(n_pos * heads, dtype=jnp.int32)[None, :]
    ok = (row // (2 * tq)) == (col % heads)
    if causal:
        ok = ok & ((col // heads) <= (row % tq))
    return jnp.where(ok, 0.0, -jnp.inf).astype(F32)


def _decode(page_table, qz, cache_k, cache_v, layer, k_new, v_new, lam_vecs, subln_g, lam_init, *, name):
    nb, n_pages = page_table.shape
    heads = cache_k.shape[3]
    page = cache_k.shape[2]
    tq = qz.shape[1] // (2 * heads)
    rows = qz.shape[1]
    vec = pl.BlockSpec((1, lam_vecs[0].shape[-1]), lambda b, p, pt: (0, 0))
    pages = _pick(n_pages, (DECODE_PAGES_PER_STEP, 2, 1))
    cache = [pl.BlockSpec((None, None, page, heads, LANES),
                          lambda b, p, pt, j=j: (layer, pt[b * n_pages + p * pages + j], 0, 0, 0))
             for j in range(pages)]
    new_rows = k_new.shape[1]
    assert new_rows % LANES == 0 and new_rows // heads >= tq
    new = pl.BlockSpec((None, new_rows, LANES), lambda b, p, pt: (b, 0, 0))
    grid_spec = pltpu.PrefetchScalarGridSpec(
        num_scalar_prefetch=1,
        grid=(nb, n_pages // pages),
        in_specs=[pl.BlockSpec((None, rows, LANES), lambda b, p, pt: (b, 0, 0)),
                  *cache, *cache, new, new,
                  pl.BlockSpec((rows, page * heads), lambda b, p, pt: (0, 0)),
                  pl.BlockSpec((rows, new_rows), lambda b, p, pt: (0, 0)),
                  vec, vec, vec, vec,
                  pl.BlockSpec((1, LANES), lambda b, p, pt: (0, 0))],
        out_specs=pl.BlockSpec((None, heads, tq, LANES), lambda b, p, pt: (b, 0, 0, 0)),
        scratch_shapes=[pltpu.VMEM((rows, LANES), F32)] * 3,
    )
    return pl.pallas_call(
        functools.partial(_decode_kernel, heads=heads, tq=tq, lam_init=lam_init, pages=pages),
        grid_spec=grid_spec,
        out_shape=jax.ShapeDtypeStruct((nb, heads, tq, LANES), F32),
        compiler_params=_params(("arbitrary", "arbitrary"), 32),
        name=name,
    )(page_table.reshape(-1), qz, *([cache_k] * pages), *([cache_v] * pages), k_new, v_new,
      _decode_bias(heads, tq, page, False), _decode_bias(heads, tq, new_rows // heads, True),
      *lam_vecs, subln_g.reshape(1, LANES))


CONV_HALO = 32


def _conv_kernel(al_ref, ag_ref, st_ref, w_ref, b_ref, g_ref, be_ref, y_ref, so_ref, ext,
                 *, tt, taps, t_valid):
    t = pl.program_id(1)
    nt = pl.num_programs(1)

    @pl.when(t == 0)
    def _():
        ext[0:CONV_HALO, :] = st_ref[...]

    ext[CONV_HALO:CONV_HALO + tt, :] = al_ref[...] * jax.nn.sigmoid(ag_ref[...])
    off = CONV_HALO - (taps - 1)
    acc = ext[pl.ds(off, tt), :] * w_ref[0:1, :]
    for j in range(1, taps):
        acc = acc + ext[pl.ds(off + j, tt), :] * w_ref[j:j + 1, :]
    acc = acc + b_ref[...]
    mu = jnp.mean(acc, axis=-1, keepdims=True)
    zc = acc - mu
    var = jnp.mean(zc * zc, axis=-1, keepdims=True)
    z = zc * lax.rsqrt(var + LN_EPS) * g_ref[...] + be_ref[...]
    y_ref[...] = (z * jax.nn.sigmoid(z)).astype(BF16)

    @pl.when(t == nt - 1)
    def _():
        so_ref[...] = ext[pl.ds(t_valid, CONV_HALO), :]

    if tt >= CONV_HALO:
        @pl.when(t < nt - 1)
        def _():
            ext[0:CONV_HALO, :] = ext[tt:tt + CONV_HALO, :]


def _conv(h, nb, t, t_valid, w, state, conv_w, conv_b, ln_g, ln_b, *, name):
    taps = conv_w.shape[0]
    tt = _pick(t, TIME_TILES)
    nt = t // tt
    assert nt == 1 or tt >= CONV_HALO
    hist = taps - 1
    st = jnp.pad(state, ((0, 0), (CONV_HALO - hist, 0), (0, 0)))
    vec = pl.BlockSpec((1, w), lambda b, i: (0, 0))
    y, so = pl.pallas_call(
        functools.partial(_conv_kernel, tt=tt, taps=taps, t_valid=t_valid - (nt - 1) * tt),
        grid=(nb, nt),
        in_specs=[pl.BlockSpec((tt, w), lambda b, i: (b * nt + i, 0)),
                  pl.BlockSpec((tt, w), lambda b, i: (b * nt + i, 1)),
                  pl.BlockSpec((None, CONV_HALO, w), lambda b, i: (b, 0, 0)),
                  pl.BlockSpec((taps, w), lambda b, i: (0, 0)), vec, vec, vec],
        out_specs=[pl.BlockSpec((tt, w), lambda b, i: (b * nt + i, 0)),
                   pl.BlockSpec((None, CONV_HALO, w), lambda b, i: (b, 0, 0))],
        out_shape=[jax.ShapeDtypeStruct((nb * t, w), BF16),
                   jax.ShapeDtypeStruct((nb, CONV_HALO, w), F32)],
        scratch_shapes=[pltpu.VMEM((CONV_HALO + tt, w), F32)],
        compiler_params=_params(("arbitrary", "arbitrary"), 32),
        name=name,
    )(h, h, st, conv_w, conv_b.reshape(1, w), ln_g.reshape(1, w), ln_b.reshape(1, w))
    return y, so[:, CONV_HALO - hist:]


POOL_HALO = 16


def _pool_kernel(c_ref, st_ref, pw_ref, sc_ref, y_ref, so_ref, ext, *, tt, t_valid, pos0, group):
    t = pl.program_id(1)
    nt = pl.num_programs(1)

    @pl.when(t == 0)
    def _():
        ext[0:POOL_HALO, :] = st_ref[...]

    ext[POOL_HALO:POOL_HALO + tt, :] = c_ref[...]
    pos = pos0 + t * tt + lax.broadcasted_iota(jnp.int32, (tt, 1), 0)
    for g, win in enumerate(POOL_WINDOWS):
        cols = pl.ds(g * group, group)
        tot = ext[pl.ds(POOL_HALO, tt), cols]
        for k in range(1, win):
            tot = tot + ext[pl.ds(POOL_HALO - k, tt), cols]
        cnt = jnp.minimum(pos + 1, win).astype(F32)
        pooled = tot / cnt - c_ref[:, cols]
        out = jnp.dot(pooled.astype(BF16), pw_ref[g].astype(BF16), preferred_element_type=F32)
        y_ref[:, cols] = (out * sc_ref[:, cols]).astype(BF16)

    @pl.when(t == nt - 1)
    def _():
        so_ref[...] = ext[pl.ds(t_valid, POOL_HALO), :]

    if tt >= POOL_HALO:
        @pl.when(t < nt - 1)
        def _():
            ext[0:POOL_HALO, :] = ext[tt:tt + POOL_HALO, :]


def _pool(h, nb, t, t_valid, w, pos0, state, pool_w, pool_scale, *, name):
    tt = _pick(t, TIME_TILES)
    nt = t // tt
    assert nt == 1 or tt >= POOL_HALO
    hist = max(POOL_WINDOWS) - 1
    groups, group = pool_w.shape[0], pool_w.shape[1]
    assert groups == len(POOL_WINDOWS) and groups * group == w
    st = jnp.pad(state, ((0, 0), (POOL_HALO - hist, 0), (0, 0)))
    ccol = 5
    y, so = pl.pallas_call(
        functools.partial(_pool_kernel, tt=tt, t_valid=t_valid - (nt - 1) * tt, pos0=pos0, group=group),
        grid=(nb, nt),
        in_specs=[pl.BlockSpec((tt, w), lambda b, i: (b * nt + i, ccol)),
                  pl.BlockSpec((None, POOL_HALO, w), lambda b, i: (b, 0, 0)),
                  pl.BlockSpec((groups, group, group), lambda b, i: (0, 0, 0)),
                  pl.BlockSpec((1, w), lambda b, i: (0, 0))],
        out_specs=[pl.BlockSpec((tt, w), lambda b, i: (b * nt + i, 0)),
                   pl.BlockSpec((None, POOL_HALO, w), lambda b, i: (b, 0, 0))],
        out_shape=[jax.ShapeDtypeStruct((nb * t, w), BF16),
                   jax.ShapeDtypeStruct((nb, POOL_HALO, w), F32)],
        scratch_shapes=[pltpu.VMEM((POOL_HALO + tt, w), F32)],
        compiler_params=_params(("arbitrary", "arbitrary"), 32),
        name=name,
    )(h, st, pool_w, pool_scale.reshape(1, w))
    return y, so[:, POOL_HALO - hist:]


def _sgu_kernel(u_ref, v_ref, g_ref, b_ref, ws_ref, bs_ref, y_ref, vn_ref, *, chunk, head):
    v = v_ref[...]
    mu = jnp.mean(v, axis=-1, keepdims=True)
    zc = v - mu
    var = jnp.mean(zc * zc, axis=-1, keepdims=True)
    vn = zc * lax.rsqrt(var + LN_EPS) * g_ref[...] + b_ref[...]
    vn_ref[...] = vn
    row = lax.broadcasted_iota(jnp.int32, (chunk, chunk), 0)
    col = lax.broadcasted_iota(jnp.int32, (chunk, chunk), 1)
    for g in range(ws_ref.shape[0]):
        cols = pl.ds(g * head, head)
        wg = jnp.where(col <= row, ws_ref[g], 0.0)
        vg = vn[:, g * head:(g + 1) * head]
        if chunk % LANES == 0:
            s = jnp.dot(wg.astype(BF16), vg.astype(BF16), preferred_element_type=F32)
        else:
            s = wg[:, 0:1] * vg[0:1, :]
            for j in range(1, chunk):
                s = s + wg[:, j:j + 1] * vg[j:j + 1, :]
        s = s + bs_ref[:, g:g + 1]
        y_ref[:, cols] = (u_ref[:, cols] * s).astype(BF16)


def _sgu(h, nb, t, w, ln_g, ln_b, sgu_w, sgu_b, *, name):
    chunk = min(t, sgu_w.shape[-1])
    nc = t // chunk
    n_heads = sgu_w.shape[0]
    head = w // n_heads
    ws = sgu_w[:, :chunk, :chunk]
    bs = sgu_b[:, :chunk].T
    vec = pl.BlockSpec((1, w), lambda b, i: (0, 0))
    return pl.pallas_call(
        functools.partial(_sgu_kernel, chunk=chunk, head=head),
        grid=(nb, nc),
        in_specs=[pl.BlockSpec((chunk, w), lambda b, i: (b * nc + i, 6)),
                  pl.BlockSpec((chunk, w), lambda b, i: (b * nc + i, 7)),
                  vec, vec,
                  pl.BlockSpec((n_heads, chunk, chunk), lambda b, i: (0, 0, 0)),
                  pl.BlockSpec((chunk, n_heads), lambda b, i: (0, 0))],
        out_specs=[pl.BlockSpec((chunk, w), lambda b, i: (b * nc + i, 0))] * 2,
        out_shape=[jax.ShapeDtypeStruct((nb * t, w), BF16), jax.ShapeDtypeStruct((nb * t, w), F32)],
        compiler_params=_params(("arbitrary", "arbitrary"), 32),
        name=name,
    )(h, h, ln_g.reshape(1, w), ln_b.reshape(1, w), ws, bs)


def _dispatch_plan(top_i, rank, counts, n_valid, bm, n_tiles):
    n_experts = counts.shape[0]
    m, top_k = top_i.shape
    tiles_e = (counts + bm - 1) // bm
    tile_end = jnp.cumsum(tiles_e)
    tile_start = tile_end - tiles_e
    experts = jnp.arange(n_experts, dtype=jnp.int32)
    start_tok = jnp.sum(jnp.where(top_i[:, :, None] == experts, tile_start, 0), axis=-1)
    routed = (jnp.arange(m, dtype=jnp.int32) < n_valid)[:, None]
    pos = jnp.where(routed, start_tok * bm + rank, 0)
    token = jnp.broadcast_to(jnp.arange(n_valid, dtype=jnp.int32)[:, None], (n_valid, top_k))
    src = jnp.zeros((n_tiles * bm,), jnp.int32).at[pos[:n_valid].reshape(-1)].set(token.reshape(-1))
    tile = jnp.arange(n_tiles, dtype=jnp.int32)
    used = tile < tile_end[-1]
    tile_e = jnp.minimum(jnp.searchsorted(tile_end, jnp.minimum(tile, tile_end[-1] - 1), side="right"),
                         n_experts - 1).astype(jnp.int32)
    rows = jnp.clip(counts[tile_e] - (tile - tile_start[tile_e]) * bm, 0, bm)
    rows = jnp.where(used, rows, 0).astype(jnp.int32)
    return pos, src, tile_e, rows


def kernel(x_prompt, x_sample, cache_k, cache_v, page_table, state_conv, state_pool, w_in, w_out, conv_w, conv_b, conv_ln_g, conv_ln_b, lam_q1, lam_k1, lam_q2, lam_k2, subln_g, pool_w, pool_scale, sgu_ln_g, sgu_ln_b, sgu_w, sgu_b, ln1_g, ln1_b, ln2_g, ln2_b, ffn_w1, ffn_w3, ffn_w2, router_w, moe_w1, moe_w3, moe_w2):
    nb, t, d = x_prompt.shape
    nbs, ts, _ = x_sample.shape
    depth = w_in.shape[0]
    w = d // 4
    assert w_in.shape[-1] == 8 * w and w % LANES == 0
    heads = w // LANES
    dk = lam_q1.shape[-1]
    assert 2 * dk == LANES and subln_g.shape[-1] == LANES
    n_experts = router_w.shape[-1]
    top_k = 2
    past_len = page_table.shape[1] * cache_k.shape[2]
    alpha = (2 * depth) ** 0.25
    tsp = _round_up(ts, BF16_SUBLANES)
    tqd = _round_up(ts, SUBLANES)

    mp, ms = nb * t, nbs * ts
    m_valid = mp + ms
    m = _round_up(m_valid, ROW_PAD)
    x = jnp.concatenate([x_prompt.reshape(mp, d), x_sample.reshape(ms, d),
                         jnp.zeros((m - m_valid, d), F32)], axis=0)
    xb = x.astype(BF16)

    dense_bm = _pick(m, (1056, 768, 512, 256))
    dense_tiles = m // dense_bm
    dense_rows = jnp.full((dense_tiles,), dense_bm, jnp.int32)
    mean_load = top_k * m_valid / n_experts
    tiles_per_expert = max(1, round(mean_load / FFN_MAX_TILE))
    moe_bm = min(FFN_MAX_TILE, max(FFN_SUB, _round_up(math.ceil(1.05 * mean_load / tiles_per_expert), 64)))
    moe_tiles = (top_k * m_valid + n_experts * (moe_bm - 1)) // moe_bm

    pos_p = jnp.arange(t, dtype=jnp.int32)
    pos_s = past_len + jnp.arange(tsp, dtype=jnp.int32)
    conv0 = jnp.zeros((nb, conv_w.shape[1] - 1, w), F32)
    pool0 = jnp.zeros((nb, max(POOL_WINDOWS) - 1, w), F32)

    outs = [[] for _ in range(9)]
    for l in range(depth):
        lam_init = 0.8 - 0.6 * math.exp(-0.3 * l)
        lam_vecs = tuple(v[l].reshape(1, dk) for v in (lam_q1, lam_k1, lam_q2, lam_k2))
        h = _matmul(xb, w_in, l, name=f"in_proj_{l}")

        ya_p, conv_p = _conv(h, nb, t, t, w, conv0, conv_w[l], conv_b[l], conv_ln_g[l], conv_ln_b[l],
                             name=f"conv_p_{l}")
        qz_p, kb_p, vb_p, kr_p = _rope(h, nb, t, w, pos_p, dk, name=f"rope_p_{l}")
        yb_p = _flash(qz_p, kb_p, vb_p, lam_vecs, subln_g[l], lam_init, name=f"attn_p_{l}")
        yc_p, pool_p = _pool(h, nb, t, t, w, 0, pool0, pool_w[l], pool_scale[l], name=f"pool_p_{l}")
        yd_p, _ = _sgu(h, nb, t, w, sgu_ln_g[l], sgu_ln_b[l], sgu_w[l], sgu_b[l], name=f"sgu_p_{l}")

        hs = jnp.pad(h[mp:m_valid].reshape(nbs, ts, 8 * w), ((0, 0), (0, tsp - ts), (0, 0)))
        hs = hs.reshape(nbs * tsp, 8 * w)
        ya_s, conv_s = _conv(hs, nbs, tsp, ts, w, state_conv[l], conv_w[l], conv_b[l], conv_ln_g[l],
                             conv_ln_b[l], name=f"conv_s_{l}")
        qz_s, _, _, kr_s = _rope(hs, nbs, tsp, w, pos_s, dk, name=f"rope_s_{l}")
        k_new = kr_s.reshape(nbs, tsp * heads, LANES)
        v_new = hs[:, 4 * w:5 * w].reshape(nbs, tsp * heads, LANES)
        o_s = _decode(page_table, qz_s[:, :, :, :tqd].reshape(nbs, heads * 2 * tqd, LANES), cache_k, cache_v,
                      l, k_new, v_new, lam_vecs, subln_g[l], lam_init, name=f"attn_s_{l}")
        yb_s = o_s.transpose(0, 2, 1, 3).reshape(nbs, tqd, w)[:, :ts].astype(BF16)
        yc_s, pool_s = _pool(hs, nbs, tsp, ts, w, past_len, state_pool[l], pool_w[l], pool_scale[l],
                             name=f"pool_s_{l}")
        yd_s, vn_s = _sgu(hs, nbs, tsp, w, sgu_ln_g[l], sgu_ln_b[l], sgu_w[l], sgu_b[l], name=f"sgu_s_{l}")

        y_p = jnp.concatenate([ya_p, yb_p, yc_p, yd_p], axis=1)
        y_s = jnp.concatenate([ya_s.reshape(nbs, tsp, w)[:, :ts], yb_s, yc_s.reshape(nbs, tsp, w)[:, :ts],
                               yd_s.reshape(nbs, tsp, w)[:, :ts]], axis=-1).reshape(ms, d)
        ycat = jnp.concatenate([y_p, y_s, jnp.zeros((m - m_valid, d), BF16)], axis=0)
        mix = _matmul(ycat, w_out, l, name=f"out_proj_{l}")
        x, xb = _add_ln(x, [mix], ln1_g[l], ln1_b[l], alpha, name=f"ln1_{l}")

        i = l // 2
        if l % 2 == 0:
            tile_e = jnp.full((dense_tiles,), i, jnp.int32)
            f = _ffn(xb, tile_e, dense_rows, ffn_w1, ffn_w3, ffn_w2, bm=dense_bm, name=f"ffn_{l}")
            x, xb = _add_ln(x, [f], ln2_g[l], ln2_b[l], alpha, name=f"ln2_{l}")
        else:
            idx, gate, rank, counts = _router(x, router_w[i], m_valid, name=f"router_{l}")
            pos, src, tile_e, rows = _dispatch_plan(idx[:, :top_k], rank[:, :top_k], counts, m_valid,
                                                    moe_bm, moe_tiles)
            routed = _ffn(xb[src], tile_e + i * n_experts, rows,
                          *(wt.reshape((-1,) + wt.shape[2:]) for wt in (moe_w1, moe_w3, moe_w2)),
                          bm=moe_bm, name=f"moe_{l}")
            x, xb = _add_ln(x, [routed[pos[:, k]] for k in range(top_k)], ln2_g[l], ln2_b[l], alpha,
                            gate=gate, name=f"ln2_{l}")

        kp = kr_p.reshape(nb, t, heads, LANES)
        vp = h[:mp, 4 * w:5 * w].reshape(nb, t, heads, LANES)
        ks = kr_s.reshape(nbs, tsp, heads, LANES)[:, :ts]
        vs = hs[:, 4 * w:5 * w].reshape(nbs, tsp, heads, LANES)[:, :ts]
        gs = vn_s.reshape(nbs, tsp, w)[:, :ts]
        for acc, val in zip(outs, (kp, vp, ks, vs, conv_p, conv_s, pool_p, pool_s, gs)):
            acc.append(val)

    return (x[:mp].reshape(nb, t, d), x[mp:m_valid].reshape(nbs, ts, d)) + tuple(jnp.stack(o) for o in outs)
```

```python
import functools
import math

import jax
import jax.numpy as jnp
from jax import lax
from jax.experimental import pallas as pl
from jax.experimental.pallas import tpu as pltpu

F32 = jnp.float32
BF16 = jnp.bfloat16

LN_EPS = 1e-5
RMS_EPS = 1e-5
ROPE_THETA = 500000.0
POOL_WINDOWS = (2, 4, 8, 16)
LANES = 128
SUBLANES = 8
BF16_SUBLANES = 16
ROW_PAD = 256
TIME_TILES = (256, 128, BF16_SUBLANES)
MIB = 1024 * 1024


def _round_up(x, m):
    return (x + m - 1) // m * m


def _params(semantics, vmem_mib):
    return pltpu.CompilerParams(dimension_semantics=semantics, vmem_limit_bytes=vmem_mib * MIB)


def _pick(n, candidates):
    for c in candidates:
        if n % c == 0:
            return c
    raise ValueError(f"no tile in {candidates} divides {n}")


def _mm_kernel(a_ref, w_ref, o_ref, wb_ref):
    @pl.when(pl.program_id(1) == 0)
    def _():
        wb_ref[...] = w_ref[...].astype(BF16)

    o_ref[...] = jnp.dot(a_ref[...], wb_ref[...], preferred_element_type=F32)


def _matmul(a, w, layer, *, bn=512, name):
    m, k = a.shape
    n = w.shape[-1]
    bm = _pick(m, (1408, 768, 512, 256))
    return pl.pallas_call(
        _mm_kernel,
        grid=(n // bn, m // bm),
        in_specs=[pl.BlockSpec((bm, k), lambda j, i: (i, 0)),
                  pl.BlockSpec((None, k, bn), lambda j, i: (layer, 0, j))],
        out_specs=pl.BlockSpec((bm, bn), lambda j, i: (i, j)),
        out_shape=jax.ShapeDtypeStruct((m, n), F32),
        scratch_shapes=[pltpu.VMEM((k, bn), BF16)],
        compiler_params=_params(("arbitrary", "arbitrary"), 56),
        name=name,
    )(a, w)


def _add_ln_kernel(x_ref, *refs, alpha, n_terms, gated):
    y_refs = refs[:n_terms]
    gate_ref = refs[n_terms] if gated else None
    g_ref, b_ref, of_ref, ob_ref = refs[n_terms + gated:]
    y = None
    for k, y_ref in enumerate(y_refs):
        term = gate_ref[:, k:k + 1] * y_ref[...] if gated else y_ref[...]
        y = term if y is None else y + term
    z = alpha * x_ref[...] + y
    mu = jnp.mean(z, axis=-1, keepdims=True)
    zc = z - mu
    var = jnp.mean(zc * zc, axis=-1, keepdims=True)
    o = zc * lax.rsqrt(var + LN_EPS) * g_ref[...] + b_ref[...]
    of_ref[...] = o
    ob_ref[...] = o.astype(BF16)


def _add_ln(x, ys, g, b, alpha, *, gate=None, name):
    m, d = x.shape
    tm = ROW_PAD // 2 if gate is not None else ROW_PAD
    row = pl.BlockSpec((tm, d), lambda i: (i, 0))
    vec = pl.BlockSpec((1, d), lambda i: (0, 0))
    gate_args = [] if gate is None else [gate]
    gate_specs = [] if gate is None else [pl.BlockSpec((tm, LANES), lambda i: (i, 0))]
    return pl.pallas_call(
        functools.partial(_add_ln_kernel, alpha=alpha, n_terms=len(ys), gated=gate is not None),
        grid=(m // tm,),
        in_specs=[row] * (1 + len(ys)) + gate_specs + [vec, vec],
        out_specs=[row, row],
        out_shape=[jax.ShapeDtypeStruct((m, d), F32), jax.ShapeDtypeStruct((m, d), BF16)],
        compiler_params=_params(("arbitrary",), 48),
        name=name,
    )(x, *ys, *gate_args, g.reshape(1, d), b.reshape(1, d))


def _router_kernel(x_ref, w_ref, idx_ref, gate_ref, rank_ref, cnt_ref, run_sc, *, n_experts, n_valid):
    i = pl.program_id(0)
    tm = x_ref.shape[0]

    @pl.when(i == 0)
    def _():
        run_sc[...] = jnp.zeros_like(run_sc)

    logits = jnp.dot(x_ref[...], w_ref[...], preferred_element_type=F32,
                     precision=lax.Precision.HIGHEST)
    lane = lax.broadcasted_iota(jnp.int32, logits.shape, 1)
    lg = jnp.where(lane < n_experts, logits, -jnp.inf)
    m1 = jnp.max(lg, axis=-1, keepdims=True)
    i1 = jnp.min(jnp.where(lg == m1, lane, LANES), axis=-1, keepdims=True)
    lg2 = jnp.where(lane == i1, -jnp.inf, lg)
    m2 = jnp.max(lg2, axis=-1, keepdims=True)
    i2 = jnp.min(jnp.where(lg2 == m2, lane, LANES), axis=-1, keepdims=True)
    e2 = jnp.exp(m2 - m1)
    den = 1.0 + e2
    idx_ref[...] = jnp.where(lane == 0, i1, jnp.where(lane == 1, i2, 0))
    gate_ref[...] = jnp.where(lane == 0, 1.0 / den, jnp.where(lane == 1, e2 / den, 0.0))

    row = i * tm + lax.broadcasted_iota(jnp.int32, (tm, 1), 0)
    onehot = jnp.where(((lane == i1) | (lane == i2)) & (row < n_valid), 1.0, 0.0)
    r_i = lax.broadcasted_iota(jnp.int32, (tm, tm), 0)
    c_i = lax.broadcasted_iota(jnp.int32, (tm, tm), 1)
    earlier = jnp.where(c_i < r_i, 1.0, 0.0).astype(BF16)
    before = jnp.dot(earlier, onehot.astype(BF16), preferred_element_type=F32) + run_sc[0:1, :]
    rank1 = jnp.sum(jnp.where(lane == i1, before, 0.0), axis=-1, keepdims=True)
    rank2 = jnp.sum(jnp.where(lane == i2, before, 0.0), axis=-1, keepdims=True)
    rank_ref[...] = jnp.where(lane == 0, rank1, jnp.where(lane == 1, rank2, 0.0)).astype(jnp.int32)
    run_sc[...] = run_sc[...] + jnp.sum(onehot, axis=0, keepdims=True)
    cnt_ref[...] = run_sc[...]


def _router(x, router_w, n_valid, *, name):
    m, d = x.shape
    e = router_w.shape[-1]
    w = jnp.zeros((d, LANES), F32).at[:, :e].set(router_w)
    tm = ROW_PAD
    tok = pl.BlockSpec((tm, LANES), lambda i: (i, 0))
    idx, gate, rank, cnt = pl.pallas_call(
        functools.partial(_router_kernel, n_experts=e, n_valid=n_valid),
        grid=(m // tm,),
        in_specs=[pl.BlockSpec((tm, d), lambda i: (i, 0)), pl.BlockSpec((d, LANES), lambda i: (0, 0))],
        out_specs=[tok, tok, tok, pl.BlockSpec((SUBLANES, LANES), lambda i: (0, 0))],
        out_shape=[jax.ShapeDtypeStruct((m, LANES), jnp.int32), jax.ShapeDtypeStruct((m, LANES), F32),
                   jax.ShapeDtypeStruct((m, LANES), jnp.int32), jax.ShapeDtypeStruct((SUBLANES, LANES), F32)],
        scratch_shapes=[pltpu.VMEM((SUBLANES, LANES), F32)],
        compiler_params=_params(("arbitrary",), 48),
        name=name,
    )(x, w)
    return idx, gate, rank, cnt[0, :e].astype(jnp.int32)


FFN_SUB = 256
FFN_OUT_CHUNK = 512
FFN_MAX_TILE = 1088


def _ffn_kernel(te_ref, tr_ref, x_ref, w1_ref, w3_ref, w2_ref, o_hbm,
                w1b, w3b, w2b, acc, sem, *, bm):
    i = pl.program_id(0)
    f = pl.program_id(1)
    rows = tr_ref[i]
    d = acc.shape[1]

    @pl.when(f == 0)
    def _():
        acc[...] = jnp.zeros_like(acc)

    def cast_weights():
        w1b[...] = w1_ref[...].astype(BF16)
        w3b[...] = w3_ref[...].astype(BF16)
        w2b[...] = w2_ref[...].astype(BF16)

    def swiglu_rows(rs):
        xs = x_ref[rs, :]
        h1 = jnp.dot(xs, w1b[...], preferred_element_type=F32)
        h3 = jnp.dot(xs, w3b[...], preferred_element_type=F32)
        h = (h1 * jax.nn.sigmoid(h1) * h3).astype(BF16)
        for c in range(0, d, FFN_OUT_CHUNK):
            cs = pl.ds(c, FFN_OUT_CHUNK)
            acc[rs, cs] += jnp.dot(h, w2b[:, cs], preferred_element_type=F32)

    @pl.when(rows > bm - FFN_SUB)
    def _():
        cast_weights()
        swiglu_rows(pl.ds(0, bm))

    @pl.when((rows > 0) & (rows <= bm - FFN_SUB))
    def _():
        cast_weights()
        for r in range(pl.cdiv(bm - FFN_SUB, FFN_SUB)):
            @pl.when(r * FFN_SUB < rows)
            def _():
                swiglu_rows(pl.ds(r * FFN_SUB, FFN_SUB))

    @pl.when(f == pl.num_programs(1) - 1)
    def _():
        cp = pltpu.make_async_copy(acc, o_hbm.at[pl.ds(i * bm, bm), :], sem)
        cp.start()
        cp.wait()


def _ffn(x, tile_expert, tile_rows, w1, w3, w2, *, bm, name):
    n_tiles = tile_expert.shape[0]
    d = x.shape[1]
    ff = w1.shape[-1]
    bf = 256
    nf = ff // bf
    assert ff % bf == 0 and bm % BF16_SUBLANES == 0 and x.shape[0] >= n_tiles * bm

    def w13_map(i, f, te, tr):
        return (te[i], 0, jnp.where(tr[i] > 0, f, nf - 1))

    def w2_map(i, f, te, tr):
        return (te[i], jnp.where(tr[i] > 0, f, nf - 1), 0)

    grid_spec = pltpu.PrefetchScalarGridSpec(
        num_scalar_prefetch=2,
        grid=(n_tiles, nf),
        in_specs=[
            pl.BlockSpec((bm, d), lambda i, f, te, tr: (i, 0), pipeline_mode=pl.Buffered(1)),
            pl.BlockSpec((None, d, bf), w13_map),
            pl.BlockSpec((None, d, bf), w13_map),
            pl.BlockSpec((None, bf, d), w2_map),
        ],
        out_specs=pl.BlockSpec(memory_space=pl.ANY),
        scratch_shapes=[pltpu.VMEM((d, bf), BF16), pltpu.VMEM((d, bf), BF16), pltpu.VMEM((bf, d), BF16),
                        pltpu.VMEM((bm, d), F32), pltpu.SemaphoreType.DMA(())],
    )
    return pl.pallas_call(
        functools.partial(_ffn_kernel, bm=bm),
        grid_spec=grid_spec,
        out_shape=jax.ShapeDtypeStruct((n_tiles * bm, d), F32),
        compiler_params=_params(("arbitrary", "arbitrary"), 60),
        name=name,
    )(tile_expert, tile_rows, x, w1, w3, w2)


def _rope_kernel(q_ref, k_ref, v_ref, cos_ref, sup_ref, sdn_ref, qz_ref, kb_ref, vb_ref, kr_ref,
                 *, dk, scale):
    def rot(x):
        return (x * cos_ref[...] + pltpu.roll(x, LANES - 8, 1) * sup_ref[...]
                + pltpu.roll(x, 8, 1) * sdn_ref[...])

    lane = lax.broadcasted_iota(jnp.int32, cos_ref.shape, 1)
    for hd in range(qz_ref.shape[0]):
        cols = pl.ds(hd * LANES, LANES)
        q = rot(q_ref[:, cols]) * scale
        k = rot(k_ref[:, cols])
        qz_ref[hd, 0] = jnp.where(lane < dk, q, 0.0).astype(BF16)
        qz_ref[hd, 1] = jnp.where(lane >= dk, q, 0.0).astype(BF16)
        kr_ref[:, cols] = k
        kb_ref[hd] = k.astype(BF16)
        vb_ref[hd] = v_ref[:, cols].astype(BF16)


def _rope_tables(pos, dk):
    rope_dim = dk // 4
    half = rope_dim // 2
    inv_freq = ROPE_THETA ** (-jnp.arange(half, dtype=F32) / half)
    ang = pos.astype(F32)[:, None] * inv_freq[None, :]
    cos, sin = jnp.cos(ang), jnp.sin(ang)
    t = pos.shape[0]
    ones = jnp.ones((t, dk - rope_dim), F32)
    zeros = jnp.zeros((t, dk - rope_dim), F32)
    zh = jnp.zeros((t, half), F32)
    c = jnp.concatenate([cos, cos, ones], axis=1)
    up = jnp.concatenate([-sin, zh, zeros], axis=1)
    dn = jnp.concatenate([zh, sin, zeros], axis=1)
    return tuple(jnp.concatenate([a, a], axis=1) for a in (c, up, dn))


def _rope(h, nb, t, w, pos, dk, *, name):
    heads = w // LANES
    tt = _pick(t, TIME_TILES)
    nt = t // tt
    cos, sup, sdn = _rope_tables(pos, dk)
    tab = pl.BlockSpec((tt, LANES), lambda b, i: (i, 0))
    col = [pl.BlockSpec((tt, w), lambda b, i, c=c: (b * nt + i, c)) for c in (2, 3, 4)]
    return pl.pallas_call(
        functools.partial(_rope_kernel, dk=dk, scale=math.log2(math.e) / math.sqrt(dk)),
        grid=(nb, nt),
        in_specs=[*col, tab, tab, tab],
        out_specs=[pl.BlockSpec((None, heads, 2, tt, LANES), lambda b, i: (b, 0, 0, i, 0)),
                   pl.BlockSpec((None, heads, tt, LANES), lambda b, i: (b, 0, i, 0)),
                   pl.BlockSpec((None, heads, tt, LANES), lambda b, i: (b, 0, i, 0)),
                   pl.BlockSpec((tt, w), lambda b, i: (b * nt + i, 0))],
        out_shape=[jax.ShapeDtypeStruct((nb, heads, 2, t, LANES), BF16),
                   jax.ShapeDtypeStruct((nb, heads, t, LANES), BF16),
                   jax.ShapeDtypeStruct((nb, heads, t, LANES), BF16),
                   jax.ShapeDtypeStruct((nb * t, w), F32)],
        compiler_params=_params(("arbitrary", "arbitrary"), 32),
        name=name,
    )(h, h, h, cos, sup, sdn)


def _lambda(lq1_ref, lk1_ref, lq2_ref, lk2_ref, lam_init):
    s1 = jnp.sum(lq1_ref[...] * lk1_ref[...], axis=-1, keepdims=True)
    s2 = jnp.sum(lq2_ref[...] * lk2_ref[...], axis=-1, keepdims=True)
    return jnp.exp(s1) - jnp.exp(s2) + lam_init


def _online_softmax_step(s, v, m_sc, l_sc, acc_sc):
    cols = [s[:, j:j + LANES] for j in range(0, s.shape[1], LANES)]
    m_prev = m_sc[...]
    m_loc = functools.reduce(jnp.maximum, cols)
    m_new = jnp.maximum(m_prev, jnp.broadcast_to(jnp.max(m_loc, axis=-1, keepdims=True), m_prev.shape))
    alpha = jnp.exp2(m_prev - m_new)
    p = [jnp.exp2(c - m_new) for c in cols]
    l_sc[...] = alpha * l_sc[...] + functools.reduce(jnp.add, p)
    pb = jnp.concatenate([c.astype(BF16) for c in p], axis=1) if len(p) > 1 else p[0].astype(BF16)
    acc_sc[...] = alpha * acc_sc[...] + jnp.dot(pb, v, preferred_element_type=F32)
    m_sc[...] = m_new


def _sub_ln(o, g, lam_init):
    return o * lax.rsqrt(jnp.mean(o * o, axis=-1, keepdims=True) + RMS_EPS) * g * (1.0 - lam_init)


def _flash_kernel(qi_ref, ki_ref, q_ref, k_ref, v_ref, lq1, lk1, lq2, lk2, g_ref, o_ref, m_sc, l_sc, acc_sc,
                  *, blk, lam_init):
    qi = qi_ref[pl.program_id(2)]
    ki = ki_ref[pl.program_id(2)]

    @pl.when(ki == 0)
    def _():
        m_sc[...] = jnp.full_like(m_sc, -jnp.inf)
        l_sc[...] = jnp.zeros_like(l_sc)
        acc_sc[...] = jnp.zeros_like(acc_sc)

    def step(diagonal):
        q = q_ref[...].reshape(2 * blk, LANES)
        s = lax.dot_general(q, k_ref[...], (((1,), (1,)), ((), ())), preferred_element_type=F32)
        if diagonal:
            row = lax.broadcasted_iota(jnp.int32, s.shape, 0)
            row = jnp.where(row >= blk, row - blk, row)
            col = lax.broadcasted_iota(jnp.int32, s.shape, 1)
            s = jnp.where(col <= row, s, -jnp.inf)
        _online_softmax_step(s, v_ref[...], m_sc, l_sc, acc_sc)

    @pl.when(ki < qi)
    def _():
        step(False)

    @pl.when(ki == qi)
    def _():
        step(True)
        lam = _lambda(lq1, lk1, lq2, lk2, lam_init)
        o = acc_sc[...] / jnp.sum(l_sc[...], axis=-1, keepdims=True)
        o = o[:blk] - lam * o[blk:]
        o_ref[...] = _sub_ln(o, g_ref[...], lam_init).astype(BF16)


def _flash(qz, kb, vb, lam_vecs, subln_g, lam_init, *, name):
    nb, heads, _, t, _ = qz.shape
    blk = _pick(t, (512, 256, 128))
    nq = t // blk
    pairs = [(qi, ki) for qi in range(nq) for ki in range(qi + 1)]
    q_of = jnp.array([p[0] for p in pairs], jnp.int32)
    k_of = jnp.array([p[1] for p in pairs], jnp.int32)
    vec = pl.BlockSpec((1, lam_vecs[0].shape[-1]), lambda b, hd, p, qo, ko: (0, 0))
    kv = pl.BlockSpec((None, None, blk, LANES), lambda b, hd, p, qo, ko: (b, hd, ko[p], 0))
    grid_spec = pltpu.PrefetchScalarGridSpec(
        num_scalar_prefetch=2,
        grid=(nb, heads, len(pairs)),
        in_specs=[pl.BlockSpec((None, None, 2, blk, LANES), lambda b, hd, p, qo, ko: (b, hd, 0, qo[p], 0)),
                  kv, kv, vec, vec, vec, vec,
                  pl.BlockSpec((1, LANES), lambda b, hd, p, qo, ko: (0, 0))],
        out_specs=pl.BlockSpec((blk, LANES), lambda b, hd, p, qo, ko: (b * nq + qo[p], hd)),
        scratch_shapes=[pltpu.VMEM((2 * blk, LANES), F32)] * 3,
    )
    return pl.pallas_call(
        functools.partial(_flash_kernel, blk=blk, lam_init=lam_init),
        grid_spec=grid_spec,
        out_shape=jax.ShapeDtypeStruct((nb * t, heads * LANES), BF16),
        compiler_params=_params(("arbitrary",) * 3, 48),
        name=name,
    )(q_of, k_of, qz, kb, vb, *lam_vecs, subln_g.reshape(1, LANES))


def _decode_kernel(pt_ref, q_ref, *refs, heads, tq, lam_init, pages):
    kc_refs, vc_refs = refs[:pages], refs[pages:2 * pages]
    kn_ref, vn_ref, bp_ref, bn_ref, lq1, lk1, lq2, lk2, g_ref, o_ref, m_sc, l_sc, acc_sc = refs[2 * pages:]
    p = pl.program_id(1)

    @pl.when(p == 0)
    def _():
        m_sc[...] = jnp.full_like(m_sc, -jnp.inf)
        l_sc[...] = jnp.zeros_like(l_sc)
        acc_sc[...] = jnp.zeros_like(acc_sc)

    q = q_ref[...]

    def update(k, v, bias):
        s = lax.dot_general(q, k.astype(BF16), (((1,), (1,)), ((), ())), preferred_element_type=F32)
        _online_softmax_step(s + bias, v.astype(BF16), m_sc, l_sc, acc_sc)

    page_rows = kc_refs[0].shape[0] * kc_refs[0].shape[1]
    for kc_ref, vc_ref in zip(kc_refs, vc_refs):
        update(kc_ref[...].reshape(page_rows, LANES), vc_ref[...].reshape(page_rows, LANES), bp_ref[...])

    @pl.when(p == pl.num_programs(1) - 1)
    def _():
        update(kn_ref[...], vn_ref[...], bn_ref[...])
        lam = _lambda(lq1, lk1, lq2, lk2, lam_init)
        o = (acc_sc[...] / jnp.sum(l_sc[...], axis=-1, keepdims=True)).reshape(heads, 2, tq, LANES)
        o = o[:, 0] - lam * o[:, 1]
        o_ref[...] = _sub_ln(o, g_ref[...], lam_init)


DECODE_PAGES_PER_STEP = 4


def _decode_bias(heads, tq, n_pos, causal):
    row = jnp.arange(heads * 2 * tq, dtype=jnp.int32)[:, None]
    col = jnp.arange(n_pos * heads, dtype=jnp.int32)[None, :]
    ok = (row // (2 * tq)) == (col % heads)
    if causal:
        ok = ok & ((col // heads) <= (row % tq))
    return jnp.where(ok, 0.0, -jnp.inf).astype(F32)


def _decode(page_table, qz, cache_k, cache_v, layer, k_new, v_new, lam_vecs, subln_g, lam_init, *, name):
    nb, n_pages = page_table.shape
    heads = cache_k.shape[3]
    page = cache_k.shape[2]
    tq = qz.shape[1] // (2 * heads)
    rows = qz.shape[1]
    vec = pl.BlockSpec((1, lam_vecs[0].shape[-1]), lambda b, p, pt: (0, 0))
    pages = _pick(n_pages, (DECODE_PAGES_PER_STEP, 2, 1))
    cache = [pl.BlockSpec((None, None, page, heads, LANES),
                          lambda b, p, pt, j=j: (layer, pt[b * n_pages + p * pages + j], 0, 0, 0))
             for j in range(pages)]
    new_rows = k_new.shape[1]
    assert new_rows % LANES == 0 and new_rows // heads >= tq
    new = pl.BlockSpec((None, new_rows, LANES), lambda b, p, pt: (b, 0, 0))
    grid_spec = pltpu.PrefetchScalarGridSpec(
        num_scalar_prefetch=1,
        grid=(nb, n_pages // pages),
        in_specs=[pl.BlockSpec((None, rows, LANES), lambda b, p, pt: (b, 0, 0)),
                  *cache, *cache, new, new,
                  pl.BlockSpec((rows, page * heads), lambda b, p, pt: (0, 0)),
                  pl.BlockSpec((rows, new_rows), lambda b, p, pt: (0, 0)),
                  vec, vec, vec, vec,
                  pl.BlockSpec((1, LANES), lambda b, p, pt: (0, 0))],
        out_specs=pl.BlockSpec((None, heads, tq, LANES), lambda b, p, pt: (b, 0, 0, 0)),
        scratch_shapes=[pltpu.VMEM((rows, LANES), F32)] * 3,
    )
    return pl.pallas_call(
        functools.partial(_decode_kernel, heads=heads, tq=tq, lam_init=lam_init, pages=pages),
        grid_spec=grid_spec,
        out_shape=jax.ShapeDtypeStruct((nb, heads, tq, LANES), F32),
        compiler_params=_params(("arbitrary", "arbitrary"), 32),
        name=name,
    )(page_table.reshape(-1), qz, *([cache_k] * pages), *([cache_v] * pages), k_new, v_new,
      _decode_bias(heads, tq, page, False), _decode_bias(heads, tq, new_rows // heads, True),
      *lam_vecs, subln_g.reshape(1, LANES))


CONV_HALO = 32


def _conv_kernel(al_ref, ag_ref, st_ref, w_ref, b_ref, g_ref, be_ref, y_ref, so_ref, ext,
                 *, tt, taps, t_valid):
    t = pl.program_id(1)
    nt = pl.num_programs(1)

    @pl.when(t == 0)
    def _():
        ext[0:CONV_HALO, :] = st_ref[...]

    ext[CONV_HALO:CONV_HALO + tt, :] = al_ref[...] * jax.nn.sigmoid(ag_ref[...])
    off = CONV_HALO - (taps - 1)
    acc = ext[pl.ds(off, tt), :] * w_ref[0:1, :]
    for j in range(1, taps):
        acc = acc + ext[pl.ds(off + j, tt), :] * w_ref[j:j + 1, :]
    acc = acc + b_ref[...]
    mu = jnp.mean(acc, axis=-1, keepdims=True)
    zc = acc - mu
    var = jnp.mean(zc * zc, axis=-1, keepdims=True)
    z = zc * lax.rsqrt(var + LN_EPS) * g_ref[...] + be_ref[...]
    y_ref[...] = (z * jax.nn.sigmoid(z)).astype(BF16)

    @pl.when(t == nt - 1)
    def _():
        so_ref[...] = ext[pl.ds(t_valid, CONV_HALO), :]

    if tt >= CONV_HALO:
        @pl.when(t < nt - 1)
        def _():
            ext[0:CONV_HALO, :] = ext[tt:tt + CONV_HALO, :]


def _conv(h, nb, t, t_valid, w, state, conv_w, conv_b, ln_g, ln_b, *, name):
    taps = conv_w.shape[0]
    tt = _pick(t, TIME_TILES)
    nt = t // tt
    assert nt == 1 or tt >= CONV_HALO
    hist = taps - 1
    st = jnp.pad(state, ((0, 0), (CONV_HALO - hist, 0), (0, 0)))
    vec = pl.BlockSpec((1, w), lambda b, i: (0, 0))
    y, so = pl.pallas_call(
        functools.partial(_conv_kernel, tt=tt, taps=taps, t_valid=t_valid - (nt - 1) * tt),
        grid=(nb, nt),
        in_specs=[pl.BlockSpec((tt, w), lambda b, i: (b * nt + i, 0)),
                  pl.BlockSpec((tt, w), lambda b, i: (b * nt + i, 1)),
                  pl.BlockSpec((None, CONV_HALO, w), lambda b, i: (b, 0, 0)),
                  pl.BlockSpec((taps, w), lambda b, i: (0, 0)), vec, vec, vec],
        out_specs=[pl.BlockSpec((tt, w), lambda b, i: (b * nt + i, 0)),
                   pl.BlockSpec((None, CONV_HALO, w), lambda b, i: (b, 0, 0))],
        out_shape=[jax.ShapeDtypeStruct((nb * t, w), BF16),
                   jax.ShapeDtypeStruct((nb, CONV_HALO, w), F32)],
        scratch_shapes=[pltpu.VMEM((CONV_HALO + tt, w), F32)],
        compiler_params=_params(("arbitrary", "arbitrary"), 32),
        name=name,
    )(h, h, st, conv_w, conv_b.reshape(1, w), ln_g.reshape(1, w), ln_b.reshape(1, w))
    return y, so[:, CONV_HALO - hist:]


POOL_HALO = 16


def _pool_kernel(c_ref, st_ref, pw_ref, sc_ref, y_ref, so_ref, ext, *, tt, t_valid, pos0, group):
    t = pl.program_id(1)
    nt = pl.num_programs(1)

    @pl.when(t == 0)
    def _():
        ext[0:POOL_HALO, :] = st_ref[...]

    ext[POOL_HALO:POOL_HALO + tt, :] = c_ref[...]
    pos = pos0 + t * tt + lax.broadcasted_iota(jnp.int32, (tt, 1), 0)
    for g, win in enumerate(POOL_WINDOWS):
        cols = pl.ds(g * group, group)
        tot = ext[pl.ds(POOL_HALO, tt), cols]
        for k in range(1, win):
            tot = tot + ext[pl.ds(POOL_HALO - k, tt), cols]
        cnt = jnp.minimum(pos + 1, win).astype(F32)
        pooled = tot / cnt - c_ref[:, cols]
        out = jnp.dot(pooled.astype(BF16), pw_ref[g].astype(BF16), preferred_element_type=F32)
        y_ref[:, cols] = (out * sc_ref[:, cols]).astype(BF16)

    @pl.when(t == nt - 1)
    def _():
        so_ref[...] = ext[pl.ds(t_valid, POOL_HALO), :]

    if tt >= POOL_HALO:
        @pl.when(t < nt - 1)
        def _():
            ext[0:POOL_HALO, :] = ext[tt:tt + POOL_HALO, :]


def _pool(h, nb, t, t_valid, w, pos0, state, pool_w, pool_scale, *, name):
    tt = _pick(t, TIME_TILES)
    nt = t // tt
    assert nt == 1 or tt >= POOL_HALO
    hist = max(POOL_WINDOWS) - 1
    groups, group = pool_w.shape[0], pool_w.shape[1]
    assert groups == len(POOL_WINDOWS) and groups * group == w
    st = jnp.pad(state, ((0, 0), (POOL_HALO - hist, 0), (0, 0)))
    ccol = 5
    y, so = pl.pallas_call(
        functools.partial(_pool_kernel, tt=tt, t_valid=t_valid - (nt - 1) * tt, pos0=pos0, group=group),
        grid=(nb, nt),
        in_specs=[pl.BlockSpec((tt, w), lambda b, i: (b * nt + i, ccol)),
                  pl.BlockSpec((None, POOL_HALO, w), lambda b, i: (b, 0, 0)),
                  pl.BlockSpec((groups, group, group), lambda b, i: (0, 0, 0)),
                  pl.BlockSpec((1, w), lambda b, i: (0, 0))],
        out_specs=[pl.BlockSpec((tt, w), lambda b, i: (b * nt + i, 0)),
                   pl.BlockSpec((None, POOL_HALO, w), lambda b, i: (b, 0, 0))],
        out_shape=[jax.ShapeDtypeStruct((nb * t, w), BF16),
                   jax.ShapeDtypeStruct((nb, POOL_HALO, w), F32)],
        scratch_shapes=[pltpu.VMEM((POOL_HALO + tt, w), F32)],
        compiler_params=_params(("arbitrary", "arbitrary"), 32),
        name=name,
    )(h, st, pool_w, pool_scale.reshape(1, w))
    return y, so[:, POOL_HALO - hist:]


def _sgu_kernel(u_ref, v_ref, g_ref, b_ref, ws_ref, bs_ref, y_ref, vn_ref, *, chunk, head):
    v = v_ref[...]
    mu = jnp.mean(v, axis=-1, keepdims=True)
    zc = v - mu
    var = jnp.mean(zc * zc, axis=-1, keepdims=True)
    vn = zc * lax.rsqrt(var + LN_EPS) * g_ref[...] + b_ref[...]
    vn_ref[...] = vn
    row = lax.broadcasted_iota(jnp.int32, (chunk, chunk), 0)
    col = lax.broadcasted_iota(jnp.int32, (chunk, chunk), 1)
    for g in range(ws_ref.shape[0]):
        cols = pl.ds(g * head, head)
        wg = jnp.where(col <= row, ws_ref[g], 0.0)
        vg = vn[:, g * head:(g + 1) * head]
        if chunk % LANES == 0:
            s = jnp.dot(wg.astype(BF16), vg.astype(BF16), preferred_element_type=F32)
        else:
            s = wg[:, 0:1] * vg[0:1, :]
            for j in range(1, chunk):
                s = s + wg[:, j:j + 1] * vg[j:j + 1, :]
        s = s + bs_ref[:, g:g + 1]
        y_ref[:, cols] = (u_ref[:, cols] * s).astype(BF16)


def _sgu(h, nb, t, w, ln_g, ln_b, sgu_w, sgu_b, *, name):
    chunk = min(t, sgu_w.shape[-1])
    nc = t // chunk
    n_heads = sgu_w.shape[0]
    head = w // n_heads
    ws = sgu_w[:, :chunk, :chunk]
    bs = sgu_b[:, :chunk].T
    vec = pl.BlockSpec((1, w), lambda b, i: (0, 0))
    return pl.pallas_call(
        functools.partial(_sgu_kernel, chunk=chunk, head=head),
        grid=(nb, nc),
        in_specs=[pl.BlockSpec((chunk, w), lambda b, i: (b * nc + i, 6)),
                  pl.BlockSpec((chunk, w), lambda b, i: (b * nc + i, 7)),
                  vec, vec,
                  pl.BlockSpec((n_heads, chunk, chunk), lambda b, i: (0, 0, 0)),
                  pl.BlockSpec((chunk, n_heads), lambda b, i: (0, 0))],
        out_specs=[pl.BlockSpec((chunk, w), lambda b, i: (b * nc + i, 0))] * 2,
        out_shape=[jax.ShapeDtypeStruct((nb * t, w), BF16), jax.ShapeDtypeStruct((nb * t, w), F32)],
        compiler_params=_params(("arbitrary", "arbitrary"), 32),
        name=name,
    )(h, h, ln_g.reshape(1, w), ln_b.reshape(1, w), ws, bs)


def _dispatch_plan(top_i, rank, counts, n_valid, bm, n_tiles):
    n_experts = counts.shape[0]
    m, top_k = top_i.shape
    tiles_e = (counts + bm - 1) // bm
    tile_end = jnp.cumsum(tiles_e)
    tile_start = tile_end - tiles_e
    experts = jnp.arange(n_experts, dtype=jnp.int32)
    start_tok = jnp.sum(jnp.where(top_i[:, :, None] == experts, tile_start, 0), axis=-1)
    routed = (jnp.arange(m, dtype=jnp.int32) < n_valid)[:, None]
    pos = jnp.where(routed, start_tok * bm + rank, 0)
    token = jnp.broadcast_to(jnp.arange(n_valid, dtype=jnp.int32)[:, None], (n_valid, top_k))
    src = jnp.zeros((n_tiles * bm,), jnp.int32).at[pos[:n_valid].reshape(-1)].set(token.reshape(-1))
    tile = jnp.arange(n_tiles, dtype=jnp.int32)
    used = tile < tile_end[-1]
    tile_e = jnp.minimum(jnp.searchsorted(tile_end, jnp.minimum(tile, tile_end[-1] - 1), side="right"),
                         n_experts - 1).astype(jnp.int32)
    rows = jnp.clip(counts[tile_e] - (tile - tile_start[tile_e]) * bm, 0, bm)
    rows = jnp.where(used, rows, 0).astype(jnp.int32)
    return pos, src, tile_e, rows


def kernel(x_prompt, x_sample, cache_k, cache_v, page_table, state_conv, state_pool, w_in, w_out, conv_w, conv_b, conv_ln_g, conv_ln_b, lam_q1, lam_k1, lam_q2, lam_k2, subln_g, pool_w, pool_scale, sgu_ln_g, sgu_ln_b, sgu_w, sgu_b, ln1_g, ln1_b, ln2_g, ln2_b, ffn_w1, ffn_w3, ffn_w2, router_w, moe_w1, moe_w3, moe_w2):
    nb, t, d = x_prompt.shape
    nbs, ts, _ = x_sample.shape
    depth = w_in.shape[0]
    w = d // 4
    assert w_in.shape[-1] == 8 * w and w % LANES == 0
    heads = w // LANES
    dk = lam_q1.shape[-1]
    assert 2 * dk == LANES and subln_g.shape[-1] == LANES
    n_experts = router_w.shape[-1]
    top_k = 2
    past_len = page_table.shape[1] * cache_k.shape[2]
    alpha = (2 * depth) ** 0.25
    tsp = _round_up(ts, BF16_SUBLANES)
    tqd = _round_up(ts, SUBLANES)

    mp, ms = nb * t, nbs * ts
    m_valid = mp + ms
    m = _round_up(m_valid, ROW_PAD)
    x = jnp.concatenate([x_prompt.reshape(mp, d), x_sample.reshape(ms, d),
                         jnp.zeros((m - m_valid, d), F32)], axis=0)
    xb = x.astype(BF16)

    dense_bm = _pick(m, (1056, 768, 512, 256))
    dense_tiles = m // dense_bm
    dense_rows = jnp.full((dense_tiles,), dense_bm, jnp.int32)
    mean_load = top_k * m_valid / n_experts
    tiles_per_expert = max(1, round(mean_load / FFN_MAX_TILE))
    moe_bm = min(FFN_MAX_TILE, max(FFN_SUB, _round_up(math.ceil(1.05 * mean_load / tiles_per_expert), 64)))
    moe_tiles = (top_k * m_valid + n_experts * (moe_bm - 1)) // moe_bm

    pos_p = jnp.arange(t, dtype=jnp.int32)
    pos_s = past_len + jnp.arange(tsp, dtype=jnp.int32)
    conv0 = jnp.zeros((nb, conv_w.shape[1] - 1, w), F32)
    pool0 = jnp.zeros((nb, max(POOL_WINDOWS) - 1, w), F32)

    outs = [[] for _ in range(9)]
    for l in range(depth):
        lam_init = 0.8 - 0.6 * math.exp(-0.3 * l)
        lam_vecs = tuple(v[l].reshape(1, dk) for v in (lam_q1, lam_k1, lam_q2, lam_k2))
        h = _matmul(xb, w_in, l, name=f"in_proj_{l}")

        ya_p, conv_p = _conv(h, nb, t, t, w, conv0, conv_w[l], conv_b[l], conv_ln_g[l], conv_ln_b[l],
                             name=f"conv_p_{l}")
        qz_p, kb_p, vb_p, kr_p = _rope(h, nb, t, w, pos_p, dk, name=f"rope_p_{l}")
        yb_p = _flash(qz_p, kb_p, vb_p, lam_vecs, subln_g[l], lam_init, name=f"attn_p_{l}")
        yc_p, pool_p = _pool(h, nb, t, t, w, 0, pool0, pool_w[l], pool_scale[l], name=f"pool_p_{l}")
        yd_p, _ = _sgu(h, nb, t, w, sgu_ln_g[l], sgu_ln_b[l], sgu_w[l], sgu_b[l], name=f"sgu_p_{l}")

        hs = jnp.pad(h[mp:m_valid].reshape(nbs, ts, 8 * w), ((0, 0), (0, tsp - ts), (0, 0)))
        hs = hs.reshape(nbs * tsp, 8 * w)
        ya_s, conv_s = _conv(hs, nbs, tsp, ts, w, state_conv[l], conv_w[l], conv_b[l], conv_ln_g[l],
                             conv_ln_b[l], name=f"conv_s_{l}")
        qz_s, _, _, kr_s = _rope(hs, nbs, tsp, w, pos_s, dk, name=f"rope_s_{l}")
        k_new = kr_s.reshape(nbs, tsp * heads, LANES)
        v_new = hs[:, 4 * w:5 * w].reshape(nbs, tsp * heads, LANES)
        o_s = _decode(page_table, qz_s[:, :, :, :tqd].reshape(nbs, heads * 2 * tqd, LANES), cache_k, cache_v,
                      l, k_new, v_new, lam_vecs, subln_g[l], lam_init, name=f"attn_s_{l}")
        yb_s = o_s.transpose(0, 2, 1, 3).reshape(nbs, tqd, w)[:, :ts].astype(BF16)
        yc_s, pool_s = _pool(hs, nbs, tsp, ts, w, past_len, state_pool[l], pool_w[l], pool_scale[l],
                             name=f"pool_s_{l}")
        yd_s, vn_s = _sgu(hs, nbs, tsp, w, sgu_ln_g[l], sgu_ln_b[l], sgu_w[l], sgu_b[l], name=f"sgu_s_{l}")

        y_p = jnp.concatenate([ya_p, yb_p, yc_p, yd_p], axis=1)
        y_s = jnp.concatenate([ya_s.reshape(nbs, tsp, w)[:, :ts], yb_s, yc_s.reshape(nbs, tsp, w)[:, :ts],
                               yd_s.reshape(nbs, tsp, w)[:, :ts]], axis=-1).reshape(ms, d)
        ycat = jnp.concatenate([y_p, y_s, jnp.zeros((m - m_valid, d), BF16)], axis=0)
        mix = _matmul(ycat, w_out, l, name=f"out_proj_{l}")
        x, xb = _add_ln(x, [mix], ln1_g[l], ln1_b[l], alpha, name=f"ln1_{l}")

        i = l // 2
        if l % 2 == 0:
            tile_e = jnp.full((dense_tiles,), i, jnp.int32)
            f = _ffn(xb, tile_e, dense_rows, ffn_w1, ffn_w3, ffn_w2, bm=dense_bm, name=f"ffn_{l}")
            x, xb = _add_ln(x, [f], ln2_g[l], ln2_b[l], alpha, name=f"ln2_{l}")
        else:
            idx, gate, rank, counts = _router(x, router_w[i], m_valid, name=f"router_{l}")
            pos, src, tile_e, rows = _dispatch_plan(idx[:, :top_k], rank[:, :top_k], counts, m_valid,
                                                    moe_bm, moe_tiles)
            routed = _ffn(xb[src], tile_e + i * n_experts, rows,
                          *(wt.reshape((-1,) + wt.shape[2:]) for wt in (moe_w1, moe_w3, moe_w2)),
                          bm=moe_bm, name=f"moe_{l}")
            x, xb = _add_ln(x, [routed[pos[:, k]] for k in range(top_k)], ln2_g[l], ln2_b[l], alpha,
                            gate=gate, name=f"ln2_{l}")

        kp = kr_p.reshape(nb, t, heads, LANES)
        vp = h[:mp, 4 * w:5 * w].reshape(nb, t, heads, LANES)
        ks = kr_s.reshape(nbs, tsp, heads, LANES)[:, :ts]
        vs = hs[:, 4 * w:5 * w].reshape(nbs, tsp, heads, LANES)[:, :ts]
        gs = vn_s.reshape(nbs, tsp, w)[:, :ts]
        for acc, val in zip(outs, (kp, vp, ks, vs, conv_p, conv_s, pool_p, pool_s, gs)):
            acc.append(val)

    return (x[:mp].reshape(nb, t, d), x[mp:m_valid].reshape(nbs, ts, d)) + tuple(jnp.stack(o) for o in outs)
```
